```python
import math
import jax, jax.numpy as jnp
from jax import lax
import numpy as np

D_MODEL = 1024
BATCH = 4
SEQ = 4096
DEPTH = 2
DEC_BATCH = 32
DEC_SEQ = 64
PAST_LEN = 2048

CHUNK = 64
D_MIX = D_MODEL
D_A = 3 * D_MIX // 8
D_B = D_MIX // 4
D_C = D_MIX - D_A - D_B
HGRN_HEAD = 64
HGRN_HEADS = D_A // HGRN_HEAD
HGRN_BLOCK = 16
S5_GROUP = 16
S5_GROUPS = D_B // S5_GROUP
S5_STATE = 64
LRU_BLOCK = 64
LRU_HEADS = D_C // LRU_BLOCK
CONV_W = 4
LRU_C = 8.0
D_FF = -(-8 * D_MODEL // (3 * 256)) * 256
D_IN = 4 * D_A + D_B + 2 * D_C
EPS = 1e-6

kernel_name = "hymba_hgrn2_s5_rglru_stream_step"

F32 = jnp.float32


def rmsnorm(x, w):
    x32 = x.astype(F32)
    y = x32 * lax.rsqrt(jnp.mean(x32 * x32, axis=-1, keepdims=True) + EPS)
    return (y * w.astype(F32)).astype(x.dtype)


def rms32(x, w):
    return x * lax.rsqrt(jnp.mean(x * x, axis=-1, keepdims=True) + EPS) * w.astype(F32)


def hgrn2_mixer(q, f_raw, inp, g, lb, norm_w, S0):
    Bsz, L, _ = q.shape
    q = q.astype(F32)
    f_raw = f_raw.astype(F32)
    inp = inp.astype(F32)
    f = lb + (1.0 - lb) * jax.nn.sigmoid(f_raw)
    log_f = jnp.log(f)
    k = (1.0 - lb) * jax.nn.sigmoid(-f_raw)
    nb = -(-L // HGRN_BLOCK)
    pad = nb * HGRN_BLOCK - L

    def blocks(t):
        t = jnp.pad(t, ((0, 0), (0, pad), (0, 0)))
        t = t.reshape(Bsz, nb, HGRN_BLOCK, HGRN_HEADS, HGRN_HEAD)
        return t.transpose(1, 0, 3, 2, 4)

    mask = jnp.tril(jnp.ones((HGRN_BLOCK, HGRN_BLOCK), dtype=bool))
    ref = HGRN_BLOCK // 2 - 1

    def step(S, blk):
        qb, kb, vb, gb = blk
        G = jnp.cumsum(gb, axis=2)
        Gr = G[:, :, ref:ref + 1]
        att = jnp.einsum('bhtd,bhsd->bhts', qb * jnp.exp(G - Gr), kb * jnp.exp(Gr - G))
        att = jnp.where(mask, att, 0.0)
        o = (jnp.einsum('bhts,bhsv->bhtv', att, vb)
             + jnp.einsum('bhtd,bhdv->bhtv', qb * jnp.exp(G), S))
        Gl = G[:, :, -1:]
        S = (jnp.exp(Gl)[:, :, 0, :, None] * S
             + jnp.einsum('bhsd,bhsv->bhdv', kb * jnp.exp(Gl - G), vb))
        return S, o

    S_fin, o = lax.scan(step, S0.astype(F32),
                        (blocks(q), blocks(k), blocks(inp), blocks(log_f)))
    o = o.transpose(1, 0, 3, 2, 4).reshape(Bsz, nb * HGRN_BLOCK, HGRN_HEADS, HGRN_HEAD)[:, :L]
    o = o * lax.rsqrt(jnp.mean(o * o, axis=-1, keepdims=True) + EPS)
    o = o.reshape(Bsz, L, D_A) * norm_w.astype(F32) * jax.nn.silu(g.astype(F32))
    return o, S_fin


def s5_mixer(u, lam_re, lam_im, log_dt, b_re, b_im, c_re, c_im, d, w_glu, b_glu, norm_w,
             h0_re, h0_im):
    Bsz, L, _ = u.shape
    u = u.astype(F32)
    ug = u.reshape(Bsz, L, S5_GROUPS, S5_GROUP)
    dt = jnp.exp(log_dt.astype(F32))[:, None]
    lr = jnp.minimum(lam_re.astype(F32), -1e-4)
    li = lam_im.astype(F32)
    mag = jnp.exp(lr * dt)
    a_re = mag * jnp.cos(li * dt)
    a_im = mag * jnp.sin(li * dt)
    den = lr * lr + li * li
    nr = a_re - 1.0
    gam_re = (nr * lr + a_im * li) / den
    gam_im = (a_im * lr - nr * li) / den
    b_re = b_re.astype(F32)
    b_im = b_im.astype(F32)
    bbar_re = gam_re[..., None] * b_re - gam_im[..., None] * b_im
    bbar_im = gam_re[..., None] * b_im + gam_im[..., None] * b_re
    bu_re = jnp.einsum('blgn,gpn->blgp', ug, bbar_re)
    bu_im = jnp.einsum('blgn,gpn->blgp', ug, bbar_im)
    h0_re = h0_re.astype(F32)
    h0_im = h0_im.astype(F32)
    bu_re = bu_re.at[:, 0].add(a_re * h0_re - a_im * h0_im)
    bu_im = bu_im.at[:, 0].add(a_re * h0_im + a_im * h0_re)
    ar = jnp.broadcast_to(a_re, bu_re.shape)
    ai = jnp.broadcast_to(a_im, bu_re.shape)

    def combine(e1, e2):
        a1r, a1i, b1r, b1i = e1
        a2r, a2i, b2r, b2i = e2
        return (a2r * a1r - a2i * a1i, a2r * a1i + a2i * a1r,
                a2r * b1r - a2i * b1i + b2r, a2r * b1i + a2i * b1r + b2i)

    _, _, h_re, h_im = lax.associative_scan(combine, (ar, ai, bu_re, bu_im), axis=1)
    y = (jnp.einsum('blgp,gnp->blgn', h_re, c_re.astype(F32))
         - jnp.einsum('blgp,gnp->blgn', h_im, c_im.astype(F32)))
    y = y.reshape(Bsz, L, D_B) + d.astype(F32) * u
    z = jax.nn.gelu(y)
    z = z * jax.nn.sigmoid(z @ w_glu.astype(F32) + b_glu.astype(F32))
    return rms32(z, norm_w), h_re[:, -1], h_im[:, -1]


def rglru_mixer(xb, gate, conv_w, conv_b, wa, ba, wx, bx, lam, norm_w, h0, buf):
    Bsz, L, _ = xb.shape
    xp = jnp.concatenate([buf.astype(F32), xb.astype(F32)], axis=1)
    conv_w = conv_w.astype(F32)
    xc = conv_b.astype(F32) + sum(xp[:, j:j + L] * conv_w[j] for j in range(CONV_W))
    new_buf = xp[:, L:]
    xh = xc.reshape(Bsz, L, LRU_HEADS, LRU_BLOCK)
    r = jax.nn.sigmoid(jnp.einsum('blhi,hij->blhj', xh, wa.astype(F32)).reshape(Bsz, L, D_C)
                       + ba.astype(F32))
    i = jax.nn.sigmoid(jnp.einsum('blhi,hij->blhj', xh, wx.astype(F32)).reshape(Bsz, L, D_C)
                       + bx.astype(F32))
    log_a = -LRU_C * r * jax.nn.softplus(-lam.astype(F32))
    a = jnp.exp(log_a)
    b = jnp.sqrt(-jnp.expm1(2.0 * log_a)) * (i * xc)
    b = b.at[:, 0].add(a[:, 0] * h0.astype(F32))

    def combine(e1, e2):
        return (e2[0] * e1[0], e2[0] * e1[1] + e2[1])

    _, h = lax.associative_scan(combine, (a, b), axis=1)
    y = h * jax.nn.gelu(gate.astype(F32))
    return rms32(y, norm_w), h[:, -1], new_buf


def trunk(x, s_hgrn, s_s5r, s_s5i, s_lru, s_conv, P):
    sm = jax.nn.softmax(P['hgrn_lb'].astype(F32), axis=0)
    cs = jnp.cumsum(sm, axis=0)
    lbs = cs - cs[0:1]
    splits = [D_A, 2 * D_A, 3 * D_A, 4 * D_A, 4 * D_A + D_B, 4 * D_A + D_B + D_C]
    out_h, out_sr, out_si, out_l, out_c = [], [], [], [], []
    for l in range(DEPTH):
        xn = rmsnorm(x, P['norm_mix'][l])
        proj = jnp.einsum('bld,de->ble', xn, P['w_in'][l])
        q, f_raw, inp, g, u, xr, gr = jnp.split(proj, splits, axis=-1)
        oA, hA = hgrn2_mixer(q, f_raw, inp, g, lbs[l], P['hgrn_norm'][l], s_hgrn[l])
        oB, hBr, hBi = s5_mixer(u, P['s5_lam_re'][l], P['s5_lam_im'][l], P['s5_log_dt'][l],
                                P['s5_b_re'][l], P['s5_b_im'][l], P['s5_c_re'][l],
                                P['s5_c_im'][l], P['s5_d'][l], P['s5_w_glu'][l],
                                P['s5_b_glu'][l], P['s5_norm'][l], s_s5r[l], s_s5i[l])
        oC, hC, bufC = rglru_mixer(xr, gr, P['lru_conv_w'][l], P['lru_conv_b'][l],
                                   P['lru_wa'][l], P['lru_ba'][l], P['lru_wx'][l],
                                   P['lru_bx'][l], P['lru_lam'][l], P['lru_norm'][l],
                                   s_lru[l], s_conv[l])
        mix = jnp.concatenate([oA, oB, oC], axis=-1).astype(x.dtype)
        x = x + jnp.einsum('ble,ed->bld', mix, P['w_out'][l])
        xn2 = rmsnorm(x, P['norm_ffn'][l])
        hid = (jax.nn.silu(jnp.einsum('bld,df->blf', xn2, P['w_ffn_gate'][l]))
               * jnp.einsum('bld,df->blf', xn2, P['w_ffn_up'][l]))
        x = x + jnp.einsum('blf,fd->bld', hid, P['w_ffn_down'][l])
        out_h.append(hA); out_sr.append(hBr); out_si.append(hBi)
        out_l.append(hC); out_c.append(bufC)
    y = rmsnorm(x, P['norm_final'])
    return (y, jnp.stack(out_h), jnp.stack(out_sr), jnp.stack(out_si),
            jnp.stack(out_l), jnp.stack(out_c))


def setup_inputs(seed: int = 0) -> dict:
    key = jax.random.key(seed)
    ks = iter(jax.random.split(key, 48))

    def nrm(shape, scale):
        return scale * jax.random.normal(next(ks), shape, F32)

    n = jnp.arange(S5_STATE, dtype=F32)
    u_lam = jax.random.uniform(next(ks), (DEPTH, D_C), F32, 0.9, 0.999)
    s_lam = u_lam ** (1.0 / LRU_C)
    return {
        "x_prompt": nrm((BATCH, SEQ, D_MODEL), 1.0),
        "x_sample": nrm((DEC_BATCH, DEC_SEQ, D_MODEL), 1.0),
        "state_hgrn": nrm((DEPTH, DEC_BATCH, HGRN_HEADS, HGRN_HEAD, HGRN_HEAD), 0.5),
        "state_s5_re": nrm((DEPTH, DEC_BATCH, S5_GROUPS, S5_STATE), 0.5),
        "state_s5_im": nrm((DEPTH, DEC_BATCH, S5_GROUPS, S5_STATE), 0.5),
        "state_rglru": nrm((DEPTH, DEC_BATCH, D_C), 1.0),
        "cache_conv": nrm((DEPTH, DEC_BATCH, CONV_W - 1, D_C), 1.0),
        "norm_mix": 1.0 + nrm((DEPTH, D_MODEL), 0.02),
        "w_in": nrm((DEPTH, D_MODEL, D_IN), D_MODEL ** -0.5),
        "hgrn_lb": nrm((DEPTH, D_A), 1.0),
        "hgrn_norm": 1.0 + nrm((DEPTH, D_A), 0.02),
        "s5_lam_re": -0.5 + nrm((DEPTH, S5_GROUPS, S5_STATE), 0.01),
        "s5_lam_im": jnp.pi * n + nrm((DEPTH, S5_GROUPS, S5_STATE), 0.01),
        "s5_log_dt": jax.random.uniform(next(ks), (DEPTH, S5_GROUPS), F32,
                                        math.log(1e-3), math.log(1e-1)),
        "s5_b_re": nrm((DEPTH, S5_GROUPS, S5_STATE, S5_GROUP), (2 * S5_GROUP) ** -0.5),
        "s5_b_im": nrm((DEPTH, S5_GROUPS, S5_STATE, S5_GROUP), (2 * S5_GROUP) ** -0.5),
        "s5_c_re": nrm((DEPTH, S5_GROUPS, S5_GROUP, S5_STATE), (2 * S5_STATE) ** -0.5),
        "s5_c_im": nrm((DEPTH, S5_GROUPS, S5_GROUP, S5_STATE), (2 * S5_STATE) ** -0.5),
        "s5_d": nrm((DEPTH, D_B), 1.0),
        "s5_w_glu": nrm((DEPTH, D_B, D_B), D_B ** -0.5),
        "s5_b_glu": nrm((DEPTH, D_B), 0.01),
        "s5_norm": 1.0 + nrm((DEPTH, D_B), 0.02),
        "lru_conv_w": nrm((DEPTH, CONV_W, D_C), CONV_W ** -0.5),
        "lru_conv_b": nrm((DEPTH, D_C), 0.01),
        "lru_wa": nrm((DEPTH, LRU_HEADS, LRU_BLOCK, LRU_BLOCK), LRU_BLOCK ** -0.5),
        "lru_ba": nrm((DEPTH, D_C), 0.01),
        "lru_wx": nrm((DEPTH, LRU_HEADS, LRU_BLOCK, LRU_BLOCK), LRU_BLOCK ** -0.5),
        "lru_bx": nrm((DEPTH, D_C), 0.01),
        "lru_lam": jnp.log(s_lam) - jnp.log1p(-s_lam),
        "lru_norm": 1.0 + nrm((DEPTH, D_C), 0.02),
        "w_out": nrm((DEPTH, D_MIX, D_MODEL), D_MIX ** -0.5),
        "norm_ffn": 1.0 + nrm((DEPTH, D_MODEL), 0.02),
        "w_ffn_gate": nrm((DEPTH, D_MODEL, D_FF), D_MODEL ** -0.5),
        "w_ffn_up": nrm((DEPTH, D_MODEL, D_FF), D_MODEL ** -0.5),
        "w_ffn_down": nrm((DEPTH, D_FF, D_MODEL), D_FF ** -0.5),
        "norm_final": 1.0 + nrm((D_MODEL,), 0.02),
    }


def reference(x_prompt, x_sample, state_hgrn, state_s5_re, state_s5_im, state_rglru, cache_conv,
              norm_mix, w_in, hgrn_lb, hgrn_norm, s5_lam_re, s5_lam_im, s5_log_dt, s5_b_re,
              s5_b_im, s5_c_re, s5_c_im, s5_d, s5_w_glu, s5_b_glu, s5_norm, lru_conv_w,
              lru_conv_b, lru_wa, lru_ba, lru_wx, lru_bx, lru_lam, lru_norm, w_out, norm_ffn,
              w_ffn_gate, w_ffn_up, w_ffn_down, norm_final):
    P = {
        'norm_mix': norm_mix, 'w_in': w_in, 'hgrn_lb': hgrn_lb, 'hgrn_norm': hgrn_norm,
        's5_lam_re': s5_lam_re, 's5_lam_im': s5_lam_im, 's5_log_dt': s5_log_dt,
        's5_b_re': s5_b_re, 's5_b_im': s5_b_im, 's5_c_re': s5_c_re, 's5_c_im': s5_c_im,
        's5_d': s5_d, 's5_w_glu': s5_w_glu, 's5_b_glu': s5_b_glu, 's5_norm': s5_norm,
        'lru_conv_w': lru_conv_w, 'lru_conv_b': lru_conv_b, 'lru_wa': lru_wa, 'lru_ba': lru_ba,
        'lru_wx': lru_wx, 'lru_bx': lru_bx, 'lru_lam': lru_lam, 'lru_norm': lru_norm,
        'w_out': w_out, 'norm_ffn': norm_ffn, 'w_ffn_gate': w_ffn_gate, 'w_ffn_up': w_ffn_up,
        'w_ffn_down': w_ffn_down, 'norm_final': norm_final,
    }
    bp = x_prompt.shape[0]
    z_hgrn = jnp.zeros((DEPTH, bp, HGRN_HEADS, HGRN_HEAD, HGRN_HEAD), F32)
    z_s5 = jnp.zeros((DEPTH, bp, S5_GROUPS, S5_STATE), F32)
    z_lru = jnp.zeros((DEPTH, bp, D_C), F32)
    z_conv = jnp.zeros((DEPTH, bp, CONV_W - 1, D_C), F32)
    y_prompt, hgrn_p, s5re_p, s5im_p, lru_p, conv_p = trunk(
        x_prompt, z_hgrn, z_s5, z_s5, z_lru, z_conv, P)
    y_sample, hgrn_s, s5re_s, s5im_s, lru_s, conv_s = trunk(
        x_sample, state_hgrn, state_s5_re, state_s5_im, state_rglru, cache_conv, P)
    return (y_prompt, y_sample, hgrn_p, s5re_p, s5im_p, lru_p, conv_p,
            hgrn_s, s5re_s, s5im_s, lru_s, conv_s)
```

```python
import functools

import numpy as np
import jax
import jax.numpy as jnp
from jax import lax
from jax.experimental import pallas as pl
from jax.experimental.pallas import tpu as pltpu

F32 = jnp.float32
BF16 = jnp.bfloat16

D_MODEL = 1024
D_A = 384
D_B = 256
D_C = 384
HEAD = 64
HEADS = D_A // HEAD
PAIRS = HEADS // 2
PAIR_W = 2 * HEAD
S5_GROUPS = 16
S5_GROUP = 16
S5_STATE = 64
S5_W = S5_GROUPS * S5_STATE
LRU_HEADS = 6
LRU_BLOCK = 64
CONV_W = 4
LRU_C = 8.0
D_FF = 2816
D_IN = 4 * D_A + D_B + 2 * D_C
EPS = 1e-6

UNIT = 64
UNITS = 4
BLOCK = UNIT * UNITS
SUB = 16
CONV_PAD = 8

COL_Q, COL_F, COL_V, COL_G = 0, D_A, 2 * D_A, 3 * D_A
COL_U = 4 * D_A
COL_XR = COL_U + D_B
COL_GR = COL_XR + D_C

VMEM_LIMIT = 56 * 1024 * 1024


def _dot(a, b):
    return jnp.dot(a, b, preferred_element_type=F32)


def _dot_nt(a, b):
    return lax.dot_general(a, b, (((1,), (1,)), ((), ())), preferred_element_type=F32)


def _dot_tn(a, b):
    return lax.dot_general(a, b, (((0,), (0,)), ((), ())), preferred_element_type=F32)


def _neg_expm1(x):
    u = jnp.exp(x)
    near = jnp.where(u == 1.0, -x, (1.0 - u) * x / jnp.log(u))
    return jnp.where(x < -0.5, 1.0 - u, near)


def _rms(x, w):
    return x * lax.rsqrt(jnp.mean(x * x, axis=-1, keepdims=True) + EPS) * w


def _prep_kernel(lb_ref, lam_re_ref, lam_im_ref, log_dt_ref, bt_re_ref, bt_im_ref, lru_lam_ref,
                 lbs_ref, a_re_ref, a_im_ref, bbar_re_ref, bbar_im_ref, lru_c_ref):
    depth = lb_ref.shape[0]
    rows = [lb_ref[l:l + 1, :] for l in range(depth)]
    m = rows[0]
    for r in rows[1:]:
        m = jnp.maximum(m, r)
    es = [jnp.exp(r - m) for r in rows]
    tot = es[0]
    for e in es[1:]:
        tot = tot + e
    cs = None
    for l in range(depth):
        sm = es[l] / tot
        cs = sm if cs is None else cs + sm
        if l == 0:
            cs0 = cs
        lbs_ref[l:l + 1, :] = cs - cs0

    dt = jnp.exp(log_dt_ref[...])
    lr = jnp.minimum(lam_re_ref[...], -1e-4)
    li = lam_im_ref[...]
    mag = jnp.exp(lr * dt)
    a_re = mag * jnp.cos(li * dt)
    a_im = mag * jnp.sin(li * dt)
    den = lr * lr + li * li
    nr = a_re - 1.0
    gam_re = (nr * lr + a_im * li) / den
    gam_im = (a_im * lr - nr * li) / den
    a_re_ref[...] = a_re
    a_im_ref[...] = a_im
    b_re = bt_re_ref[...]
    b_im = bt_im_ref[...]
    bbar_re_ref[...] = gam_re * b_re - gam_im * b_im
    bbar_im_ref[...] = gam_re * b_im + gam_im * b_re

    z = -lru_lam_ref[...]
    softplus = jnp.maximum(z, 0.0) + jnp.log1p(jnp.exp(-jnp.abs(z)))
    lru_c_ref[...] = -LRU_C * softplus


def _prep(hgrn_lb, lam_re, lam_im, log_dt, b_re, b_im, lru_lam):
    depth = hgrn_lb.shape[0]
    g, p, n = S5_GROUPS, S5_STATE, S5_GROUP
    out_shape = (
        jax.ShapeDtypeStruct((depth, D_A), F32),
        jax.ShapeDtypeStruct((depth, g, 1, p), F32),
        jax.ShapeDtypeStruct((depth, g, 1, p), F32),
        jax.ShapeDtypeStruct((depth, g, n, p), F32),
        jax.ShapeDtypeStruct((depth, g, n, p), F32),
        jax.ShapeDtypeStruct((depth, D_C), F32),
    )
    return pl.pallas_call(_prep_kernel, out_shape=out_shape, name="prep")(
        hgrn_lb, lam_re.reshape(depth, g, 1, p), lam_im.reshape(depth, g, 1, p),
        log_dt.reshape(depth, g, 1, 1), jnp.swapaxes(b_re, -1, -2), jnp.swapaxes(b_im, -1, -2),
        lru_lam)


def _inproj_kernel(x_ref, nw_ref, w_ref, o_ref):
    xn = _rms(x_ref[...], nw_ref[...])
    o_ref[...] = _dot(xn.astype(BF16), w_ref[...])


def _inproj(x, norm_w, w_bf16, tm):
    t = x.shape[0]
    const = lambda i: (0, 0)
    return pl.pallas_call(
        _inproj_kernel,
        grid=(t // tm,),
        in_specs=[
            pl.BlockSpec((tm, D_MODEL), lambda i: (i, 0)),
            pl.BlockSpec((1, D_MODEL), const),
            pl.BlockSpec((D_MODEL, D_IN), const, pipeline_mode=pl.Buffered(1)),
        ],
        out_specs=pl.BlockSpec((tm, D_IN), lambda i: (i, 0)),
        out_shape=jax.ShapeDtypeStruct((t, D_IN), F32),
        compiler_params=pltpu.CompilerParams(
            dimension_semantics=("arbitrary",), vmem_limit_bytes=VMEM_LIMIT),
        name="inproj",
    )(x, norm_w.reshape(1, D_MODEL), w_bf16)


FF_CHUNK = D_FF // 2


def _ffn_kernel(final, mix_ref, x_ref, wo_ref, nw_ref, wg_ref, wu_ref, wd_ref, nf_ref, o_ref,
                hid_scr):
    x1 = x_ref[...] + _dot(mix_ref[...].astype(BF16), wo_ref[...])
    xn = _rms(x1, nw_ref[...]).astype(BF16)
    for c in range(D_FF // FF_CHUNK):
        cols = slice(c * FF_CHUNK, (c + 1) * FF_CHUNK)
        gate = _dot(xn, wg_ref[:, cols])
        up = _dot(xn, wu_ref[:, cols])
        hid_scr[:, cols] = (jax.nn.silu(gate) * up).astype(BF16)
    x2 = x1 + _dot(hid_scr[...], wd_ref[...])
    if final:
        o_ref[...] = _rms(x2, nf_ref[...])
    else:
        o_ref[...] = x2


def _ffn(mix, x, wo, norm_w, wg, wu, wd, norm_final, final, tm):
    t = x.shape[0]
    const = lambda i: (0, 0)
    row = lambda i: (i, 0)
    resident = functools.partial(pl.BlockSpec, index_map=const, pipeline_mode=pl.Buffered(1))
    return pl.pallas_call(
        functools.partial(_ffn_kernel, final),
        grid=(t // tm,),
        in_specs=[
            pl.BlockSpec((tm, D_MODEL), row),
            pl.BlockSpec((tm, D_MODEL), row),
            resident((D_MODEL, D_MODEL)),
            pl.BlockSpec((1, D_MODEL), const),
            resident((D_MODEL, D_FF)),
            resident((D_MODEL, D_FF)),
            resident((D_FF, D_MODEL)),
            pl.BlockSpec((1, D_MODEL), const),
        ],
        out_specs=pl.BlockSpec((tm, D_MODEL), row),
        out_shape=jax.ShapeDtypeStruct((t, D_MODEL), F32),
        scratch_shapes=[pltpu.VMEM((tm, D_FF), BF16)],
        compiler_params=pltpu.CompilerParams(
            dimension_semantics=("arbitrary",), vmem_limit_bytes=VMEM_LIMIT),
        name="ffn",
    )(mix, x, wo, norm_w.reshape(1, D_MODEL), wg, wu, wd, norm_final.reshape(1, D_MODEL))


def _split3(x):
    hi = x.astype(BF16)
    r1 = x - hi.astype(F32)
    mid = r1.astype(BF16)
    lo = (r1 - mid.astype(F32)).astype(BF16)
    return hi, mid, lo


def _bcast_rows(x, idxs, reps):
    width = x.shape[1]
    return jnp.concatenate(
        [jnp.broadcast_to(x[i:i + 1, :], (reps, width)) for i in idxs], axis=0)


def _mixer_kernel(seqs, steps_per_seq,
                  proj_ref, hg0_ref, s50_ref, lru0_ref, cv0_ref,
                  lb_ref, hnorm_ref, tri_ref, ones_ref,
                  wb_ref, are_ref, aim_ref, wc_ref, d_ref, wglu_ref, bglu_ref, snorm_ref,
                  cw_ref, cb_ref, wax_ref, bax_ref, lc_ref, lnorm_ref,
                  mix_ref, hgo_ref, s5o_ref, lruo_ref, cvo_ref,
                  st_hg, st_s5, st_lru, st_cv, bu_scr, la_scr, lb_scr, xp_scr, o_scr):
    carry = seqs == 1
    first = (pl.program_id(0) % steps_per_seq) == 0

    def load_state(u, dst, src_ref):
        if not carry:
            dst[...] = src_ref[u]
        elif u == 0:
            @pl.when(first)
            def _():
                dst[...] = src_ref[0]

    def store_state(u, dst_ref, src):
        if not carry:
            dst_ref[u] = src[...]
        elif u == UNITS - 1:
            dst_ref[0] = src[...]

    u_in = proj_ref[:, COL_U:COL_U + D_B]
    bu_scr[...] = _dot(u_in.astype(BF16), wb_ref[...])
    a_re = are_ref[...]
    a_im = aim_ref[...]
    for u in range(UNITS):
        load_state(u, st_s5, s50_ref)
        base = u * UNIT

        def s5_step(t, h):
            hr, hi = h
            r = base + t
            nr = a_re * hr - a_im * hi + bu_scr[pl.ds(r, 1), 0:S5_W]
            ni = a_re * hi + a_im * hr + bu_scr[pl.ds(r, 1), S5_W:2 * S5_W]
            bu_scr[pl.ds(r, 1), 0:S5_W] = nr
            bu_scr[pl.ds(r, 1), S5_W:2 * S5_W] = ni
            return nr, ni

        hr, hi = lax.fori_loop(0, UNIT, s5_step, (st_s5[:, 0:S5_W], st_s5[:, S5_W:2 * S5_W]),
                               unroll=8)
        st_s5[:, 0:S5_W] = hr
        st_s5[:, S5_W:2 * S5_W] = hi
        store_state(u, s5o_ref, st_s5)
    y = _dot(bu_scr[...].astype(BF16), wc_ref[...]) + d_ref[...] * u_in
    z = jax.nn.gelu(y)
    z = z * jax.nn.sigmoid(_dot(z.astype(BF16), wglu_ref[...]) + bglu_ref[...])
    mix_ref[:, D_A:D_A + D_B] = _rms(z, snorm_ref[...])

    for u in range(UNITS):
        load_state(u, st_cv, cv0_ref)
        rows = slice(u * UNIT, (u + 1) * UNIT)
        xr = proj_ref[rows, COL_XR:COL_XR + D_C]
        xp_scr[CONV_PAD - (CONV_W - 1):CONV_PAD, :] = st_cv[...]
        xp_scr[CONV_PAD:CONV_PAD + UNIT, :] = xr
        xc = cb_ref[...] + xr * cw_ref[CONV_W - 1:CONV_W, :]
        for j in range(CONV_W - 1):
            lo = CONV_PAD - (CONV_W - 1) + j
            xc = xc + xp_scr[lo:lo + UNIT, :] * cw_ref[j:j + 1, :]
        st_cv[...] = xr[UNIT - (CONV_W - 1):UNIT, :]
        store_state(u, cvo_ref, st_cv)
        lb_scr[rows, :] = xc
    xc = lb_scr[...]
    gates = _dot(xc.astype(BF16), wax_ref[...]) + bax_ref[...]
    r_gate = jax.nn.sigmoid(gates[:, 0:D_C])
    i_gate = jax.nn.sigmoid(gates[:, D_C:2 * D_C])
    log_a = lc_ref[...] * r_gate
    la_scr[...] = jnp.exp(log_a)
    lb_scr[...] = jnp.sqrt(_neg_expm1(2.0 * log_a)) * (i_gate * xc)
    for u in range(UNITS):
        load_state(u, st_lru, lru0_ref)
        base = u * UNIT

        def lru_step(t, h):
            r = base + t
            nh = la_scr[pl.ds(r, 1), :] * h + lb_scr[pl.ds(r, 1), :]
            lb_scr[pl.ds(r, 1), :] = nh
            return nh

        st_lru[...] = lax.fori_loop(0, UNIT, lru_step, st_lru[...], unroll=8)
        store_state(u, lruo_ref, st_lru)
    yc = lb_scr[...] * jax.nn.gelu(proj_ref[:, COL_GR:COL_GR + D_C])
    mix_ref[:, D_A + D_B:D_MODEL] = _rms(yc, lnorm_ref[...])

    lbv = lb_ref[...]
    rowi = lax.broadcasted_iota(jnp.int32, (UNIT, D_A), 0)
    right0 = (rowi % (2 * SUB)) >= SUB
    right1 = rowi >= 2 * SUB
    lane_p = lax.broadcasted_iota(jnp.int32, (UNIT, PAIR_W), 1)
    head0 = lane_p < HEAD
    tq = lax.broadcasted_iota(jnp.int32, (2 * UNIT, UNIT), 0) % UNIT
    sk = lax.broadcasted_iota(jnp.int32, (2 * UNIT, UNIT), 1)
    lag = tq - sk
    mask_d = lag * (lag - tq % SUB) <= 0
    mask_0 = (tq // (2 * SUB)) == (sk // (2 * SUB))
    vi = lax.broadcasted_iota(jnp.int32, (PAIR_W, PAIR_W), 0) // HEAD
    di = lax.broadcasted_iota(jnp.int32, (PAIR_W, PAIR_W), 1) // HEAD
    same_head = vi == di

    def stack_heads(x):
        return jnp.concatenate(
            [jnp.where(head0, x, 0.0), jnp.where(head0, 0.0, x)], axis=0).astype(BF16)

    for u in range(UNITS):
        load_state(u, st_hg, hg0_ref)
        rows = slice(u * UNIT, (u + 1) * UNIT)
        q = proj_ref[rows, COL_Q:COL_Q + D_A]
        fr = proj_ref[rows, COL_F:COL_F + D_A]
        v = proj_ref[rows, COL_V:COL_V + D_A]
        e = jnp.exp(-jnp.abs(fr))
        rcp = 1.0 / (1.0 + e)
        pos = fr >= 0.0
        sig = jnp.where(pos, rcp, e * rcp)
        sgm = jnp.where(pos, e * rcp, rcp)
        lf = jnp.log(lbv + (1.0 - lbv) * sig)
        k = (1.0 - lbv) * sgm
        g_cum = _dot(tri_ref[...], jnp.concatenate(_split3(lf), axis=0))
        dd = g_cum - _bcast_rows(g_cum, [SUB * i + SUB // 2 - 1 for i in range(UNIT // SUB)], SUB)
        q_d = q * jnp.exp(dd)
        k_d = k * jnp.exp(-dd)
        x0 = g_cum - _bcast_rows(g_cum, [2 * SUB * i + SUB - 1 for i in range(UNIT // (2 * SUB))],
                                 2 * SUB)
        e0 = jnp.exp(jnp.where(right0, x0, -x0))
        q_0 = jnp.where(right0, q * e0, 0.0)
        k_0 = jnp.where(right0, 0.0, k * e0)
        x1 = g_cum - g_cum[2 * SUB - 1:2 * SUB, :]
        e1 = jnp.exp(jnp.where(right1, x1, -x1))
        q_1 = jnp.where(right1, q * e1, 0.0)
        k_1 = jnp.where(right1, 0.0, k * e1)
        g_end = g_cum[UNIT - 1:UNIT, :]
        q_h = q * jnp.exp(g_cum)
        k_h = k * jnp.exp(g_end - g_cum)
        dec = jnp.exp(g_end)
        for p in range(PAIRS):
            sl = slice(p * PAIR_W, (p + 1) * PAIR_W)
            att = (jnp.where(mask_d, _dot_nt(stack_heads(q_d[:, sl]), k_d[:, sl].astype(BF16)), 0.0)
                   + jnp.where(mask_0, _dot_nt(stack_heads(q_0[:, sl]), k_0[:, sl].astype(BF16)), 0.0)
                   + _dot_nt(stack_heads(q_1[:, sl]), k_1[:, sl].astype(BF16)))
            vb = v[:, sl].astype(BF16)
            both = _dot(att.astype(BF16), vb)
            st = st_hg[p]
            o = (jnp.where(head0, both[0:UNIT], both[UNIT:2 * UNIT])
                 + _dot_nt(q_h[:, sl].astype(BF16), st.astype(BF16)))
            o_scr[rows, sl] = o
            upd = _dot_tn(vb, k_h[:, sl].astype(BF16))
            st_hg[p] = st * dec[:, sl] + jnp.where(same_head, upd, 0.0)
        store_state(u, hgo_ref, st_hg)
    o = o_scr[...]
    o2 = o * o
    o2_hi = o2.astype(BF16)
    o2_lo = (o2 - o2_hi.astype(F32)).astype(BF16)
    msq = _dot(jnp.concatenate([o2_hi, o2_lo], axis=1), ones_ref[...]) * (1.0 / HEAD)
    on = o * lax.rsqrt(msq + EPS)
    mix_ref[:, 0:D_A] = on * hnorm_ref[...] * jax.nn.silu(proj_ref[:, COL_G:COL_G + D_A])


def _mixer_constants():
    t = np.arange(UNIT)
    tri = (t[None, :] <= t[:, None]).astype(np.float32)
    tri3 = np.concatenate([tri, tri, tri], axis=1)
    h = np.arange(D_A) // HEAD
    ones = (h[:, None] == h[None, :]).astype(np.float32)
    ones2 = np.concatenate([ones, ones], axis=0)
    return jnp.asarray(tri3, BF16), jnp.asarray(ones2, BF16)


def _mixer(proj, states, lp, seqs):
    t = proj.shape[0]
    hg0, s50, lru0, cv0 = states
    nseq = hg0.shape[0]
    steps = t // BLOCK
    steps_per_seq = steps // nseq if seqs == 1 else 1
    tri3, ones2 = _mixer_constants()

    def seq_map(nd):
        if seqs == 1:
            return lambda i: (i // steps_per_seq,) + (0,) * (nd - 1)
        return lambda i: (i,) + (0,) * (nd - 1)

    def const(nd):
        return lambda i: (0,) * nd

    state_blocks = [
        pl.BlockSpec((seqs, PAIRS, PAIR_W, PAIR_W), seq_map(4)),
        pl.BlockSpec((seqs, 1, 2 * S5_W), seq_map(3)),
        pl.BlockSpec((seqs, 1, D_C), seq_map(3)),
        pl.BlockSpec((seqs, CONV_W - 1, D_C), seq_map(3)),
    ]
    params = [lp["lb"], lp["hnorm"], tri3, ones2,
              lp["wb"], lp["a_re"], lp["a_im"], lp["wc"], lp["d"], lp["wglu"], lp["bglu"],
              lp["snorm"], lp["cw"], lp["cb"], lp["wax"], lp["bax"], lp["lc"], lp["lnorm"]]
    in_specs = ([pl.BlockSpec((BLOCK, D_IN), lambda i: (i, 0))] + state_blocks
                + [pl.BlockSpec(p.shape, const(p.ndim)) for p in params])
    out_shape = (
        jax.ShapeDtypeStruct((t, D_MODEL), F32),
        jax.ShapeDtypeStruct(hg0.shape, F32),
        jax.ShapeDtypeStruct(s50.shape, F32),
        jax.ShapeDtypeStruct(lru0.shape, F32),
        jax.ShapeDtypeStruct(cv0.shape, F32),
    )
    out_specs = [pl.BlockSpec((BLOCK, D_MODEL), lambda i: (i, 0))] + state_blocks
    scratch = [
        pltpu.VMEM((PAIRS, PAIR_W, PAIR_W), F32),
        pltpu.VMEM((1, 2 * S5_W), F32),
        pltpu.VMEM((1, D_C), F32),
        pltpu.VMEM((CONV_W - 1, D_C), F32),
        pltpu.VMEM((BLOCK, 2 * S5_W), F32),
        pltpu.VMEM((BLOCK, D_C), F32),
        pltpu.VMEM((BLOCK, D_C), F32),
        pltpu.VMEM((CONV_PAD + UNIT, D_C), F32),
        pltpu.VMEM((BLOCK, D_A), F32),
    ]
    return pl.pallas_call(
        functools.partial(_mixer_kernel, seqs, steps_per_seq),
        grid=(steps,),
        in_specs=in_specs,
        out_specs=out_specs,
        out_shape=out_shape,
        scratch_shapes=scratch,
        compiler_params=pltpu.CompilerParams(
            dimension_semantics=("arbitrary",), vmem_limit_bytes=VMEM_LIMIT),
        name="mixer",
    )(proj, hg0, s50, lru0, cv0, *params)


def _block_diag(w):
    h, i, j = w.shape
    return jnp.einsum("hij,hk->hikj", w, jnp.eye(h, dtype=w.dtype)).reshape(h * i, h * j)


def _pack_hgrn(s):
    n = s.shape[0]
    st = jnp.swapaxes(s, -1, -2).reshape(n, PAIRS, 2, HEAD, HEAD)
    out = jnp.einsum("nphvd,hk->nphvkd", st, jnp.eye(2, dtype=s.dtype))
    return out.reshape(n, PAIRS, PAIR_W, PAIR_W)


def _unpack_hgrn(s):
    n = s.shape[0]
    s6 = s.reshape(n, PAIRS, 2, HEAD, 2, HEAD)
    st = jnp.stack([s6[:, :, 0, :, 0, :], s6[:, :, 1, :, 1, :]], axis=2)
    return jnp.swapaxes(st, -1, -2).reshape(n, HEADS, HEAD, HEAD)


def _layer_params(l, consts, s5_c_re, s5_c_im, s5_d, s5_w_glu, s5_b_glu, s5_norm, hgrn_norm,
                  lru_conv_w, lru_conv_b, lru_wa, lru_ba, lru_wx, lru_bx, lru_norm):
    lbs, a_re, a_im, bbar_re, bbar_im, lru_c = consts
    g = S5_GROUPS
    eye = jnp.eye(g, dtype=F32)
    wb = jnp.concatenate([
        jnp.einsum("gnp,gh->gnhp", bbar_re[l], eye).reshape(D_B, S5_W),
        jnp.einsum("gnp,gh->gnhp", bbar_im[l], eye).reshape(D_B, S5_W)], axis=1)
    wc = jnp.concatenate([
        jnp.einsum("gnp,gh->gphn", s5_c_re[l], eye).reshape(S5_W, D_B),
        jnp.einsum("gnp,gh->gphn", -s5_c_im[l], eye).reshape(S5_W, D_B)], axis=0)
    return {
        "lb": lbs[l].reshape(1, D_A), "hnorm": hgrn_norm[l].reshape(1, D_A),
        "wb": wb.astype(BF16), "a_re": a_re[l].reshape(1, S5_W), "a_im": a_im[l].reshape(1, S5_W),
        "wc": wc.astype(BF16), "d": s5_d[l].reshape(1, D_B), "wglu": s5_w_glu[l].astype(BF16),
        "bglu": s5_b_glu[l].reshape(1, D_B), "snorm": s5_norm[l].reshape(1, D_B),
        "cw": lru_conv_w[l], "cb": lru_conv_b[l].reshape(1, D_C),
        "wax": jnp.concatenate([_block_diag(lru_wa[l]), _block_diag(lru_wx[l])], axis=1).astype(BF16),
        "bax": jnp.concatenate([lru_ba[l], lru_bx[l]]).reshape(1, 2 * D_C),
        "lc": lru_c[l].reshape(1, D_C), "lnorm": lru_norm[l].reshape(1, D_C),
    }


def kernel(x_prompt, x_sample, state_hgrn, state_s5_re, state_s5_im, state_rglru, cache_conv, norm_mix, w_in, hgrn_lb, hgrn_norm, s5_lam_re, s5_lam_im, s5_log_dt, s5_b_re, s5_b_im, s5_c_re, s5_c_im, s5_d, s5_w_glu, s5_b_glu, s5_norm, lru_conv_w, lru_conv_b, lru_wa, lru_ba, lru_wx, lru_bx, lru_lam, lru_norm, w_out, norm_ffn, w_ffn_gate, w_ffn_up, w_ffn_down, norm_final):
    depth = w_in.shape[0]
    consts = _prep(hgrn_lb, s5_lam_re, s5_lam_im, s5_log_dt, s5_b_re, s5_b_im, lru_lam)
    layers = [_layer_params(l, consts, s5_c_re, s5_c_im, s5_d, s5_w_glu, s5_b_glu, s5_norm,
                            hgrn_norm, lru_conv_w, lru_conv_b, lru_wa, lru_ba, lru_wx, lru_bx,
                            lru_norm) for l in range(depth)]
    w_in_b = w_in.astype(BF16)
    w_out_b = w_out.astype(BF16)
    wg_b = w_ffn_gate.astype(BF16)
    wu_b = w_ffn_up.astype(BF16)
    wd_b = w_ffn_down.astype(BF16)

    def trunk(x, s_hgrn, s_re, s_im, s_lru, s_conv, seqs):
        nseq, length, _ = x.shape
        xf = x.reshape(nseq * length, D_MODEL)
        outs = []
        for l in range(depth):
            proj = _inproj(xf, norm_mix[l], w_in_b[l], 512)
            states = (_pack_hgrn(s_hgrn[l]),
                      jnp.concatenate([s_re[l].reshape(nseq, 1, S5_W),
                                       s_im[l].reshape(nseq, 1, S5_W)], axis=-1),
                      s_lru[l].reshape(nseq, 1, D_C), s_conv[l])
            mix, hg, s5, lru, cv = _mixer(proj, states, layers[l], seqs)
            xf = _ffn(mix, xf, w_out_b[l], norm_ffn[l], wg_b[l], wu_b[l], wd_b[l], norm_final,
                      l == depth - 1, 512)
            outs.append((_unpack_hgrn(hg),
                         s5[:, 0, 0:S5_W].reshape(nseq, S5_GROUPS, S5_STATE),
                         s5[:, 0, S5_W:].reshape(nseq, S5_GROUPS, S5_STATE),
                         lru[:, 0, :], cv))
        stacked = tuple(jnp.stack([o[i] for o in outs]) for i in range(5))
        return (xf.reshape(nseq, length, D_MODEL),) + stacked

    bp = x_prompt.shape[0]
    zeros = lambda *shape: jnp.zeros((depth, bp) + shape, F32)
    res_p = trunk(x_prompt, zeros(HEADS, HEAD, HEAD), zeros(S5_GROUPS, S5_STATE),
                  zeros(S5_GROUPS, S5_STATE), zeros(D_C), zeros(CONV_W - 1, D_C), 1)
    res_s = trunk(x_sample, state_hgrn, state_s5_re, state_s5_im, state_rglru, cache_conv, UNITS)
    return (res_p[0], res_s[0]) + res_p[1:] + res_s[1:]
```

```python
import functools

import numpy as np
import jax
import jax.numpy as jnp
from jax import lax
from jax.experimental import pallas as pl
from jax.experimental.pallas import tpu as pltpu

F32 = jnp.float32
BF16 = jnp.bfloat16

D_MODEL = 1024
D_A = 384
D_B = 256
D_C = 384
HEAD = 64
HEADS = D_A // HEAD
PAIRS = HEADS // 2
PAIR_W = 2 * HEAD
S5_GROUPS = 16
S5_GROUP = 16
S5_STATE = 64
S5_W = S5_GROUPS * S5_STATE
LRU_HEADS = 6
LRU_BLOCK = 64
CONV_W = 4
LRU_C = 8.0
D_FF = 2816
D_IN = 4 * D_A + D_B + 2 * D_C
EPS = 1e-6

LANES = 128
UNIT = 64
SEQS = 4
ROWS = UNIT * SEQS
SUB = 16
CONV_PAD = 16
S5_TILES = S5_W // LANES
LRU_TILES = D_C // LANES

COL_Q, COL_F, COL_V, COL_G = 0, D_A, 2 * D_A, 3 * D_A
COL_U = 4 * D_A
COL_XR = COL_U + D_B
COL_GR = COL_XR + D_C

VMEM_LIMIT = 56 * 1024 * 1024


def _dot(a, b):
    return jnp.dot(a, b, preferred_element_type=F32)


def _dot_nt(a, b):
    return lax.dot_general(a, b, (((1,), (1,)), ((), ())), preferred_element_type=F32)


def _dot_tn(a, b):
    return lax.dot_general(a, b, (((0,), (0,)), ((), ())), preferred_element_type=F32)


def _neg_expm1(x):
    u = jnp.exp(x)
    near = jnp.where(u == 1.0, -x, (1.0 - u) * x / jnp.log(u))
    return jnp.where(x < -0.5, 1.0 - u, near)


def _rms(x, w):
    return x * lax.rsqrt(jnp.mean(x * x, axis=-1, keepdims=True) + EPS) * w


def _layer_spec(arr, layer):
    nd = arr.ndim
    return pl.BlockSpec((None,) + arr.shape[1:], lambda *_: (layer,) + (0,) * (nd - 1),
                        pipeline_mode=pl.Buffered(1))


def _prep_kernel(lb_ref, lam_re_ref, lam_im_ref, log_dt_ref, bt_re_ref, bt_im_ref, lru_lam_ref,
                 lbs_ref, a_re_ref, a_im_ref, bbar_re_ref, bbar_im_ref, lru_c_ref):
    depth = lb_ref.shape[0]
    rows = [lb_ref[l:l + 1, :] for l in range(depth)]
    m = rows[0]
    for r in rows[1:]:
        m = jnp.maximum(m, r)
    es = [jnp.exp(r - m) for r in rows]
    tot = es[0]
    for e in es[1:]:
        tot = tot + e
    cs = None
    for l in range(depth):
        sm = es[l] / tot
        cs = sm if cs is None else cs + sm
        if l == 0:
            cs0 = cs
        lbs_ref[l:l + 1, :] = cs - cs0

    dt = jnp.exp(log_dt_ref[...])
    lr = jnp.minimum(lam_re_ref[...], -1e-4)
    li = lam_im_ref[...]
    mag = jnp.exp(lr * dt)
    a_re = mag * jnp.cos(li * dt)
    a_im = mag * jnp.sin(li * dt)
    den = lr * lr + li * li
    nr = a_re - 1.0
    gam_re = (nr * lr + a_im * li) / den
    gam_im = (a_im * lr - nr * li) / den
    a_re_ref[...] = a_re
    a_im_ref[...] = a_im
    b_re = bt_re_ref[...]
    b_im = bt_im_ref[...]
    bbar_re_ref[...] = gam_re * b_re - gam_im * b_im
    bbar_im_ref[...] = gam_re * b_im + gam_im * b_re

    z = -lru_lam_ref[...]
    softplus = jnp.maximum(z, 0.0) + jnp.log1p(jnp.exp(-jnp.abs(z)))
    lru_c_ref[...] = -LRU_C * softplus


def _prep(hgrn_lb, lam_re, lam_im, log_dt, b_re, b_im, lru_lam):
    depth = hgrn_lb.shape[0]
    g, p, n = S5_GROUPS, S5_STATE, S5_GROUP
    out_shape = (
        jax.ShapeDtypeStruct((depth, D_A), F32),
        jax.ShapeDtypeStruct((depth, g, 1, p), F32),
        jax.ShapeDtypeStruct((depth, g, 1, p), F32),
        jax.ShapeDtypeStruct((depth, g, n, p), F32),
        jax.ShapeDtypeStruct((depth, g, n, p), F32),
        jax.ShapeDtypeStruct((depth, D_C), F32),
    )
    return pl.pallas_call(_prep_kernel, out_shape=out_shape, name="prep")(
        hgrn_lb, lam_re.reshape(depth, g, 1, p), lam_im.reshape(depth, g, 1, p),
        log_dt.reshape(depth, g, 1, 1), jnp.swapaxes(b_re, -1, -2), jnp.swapaxes(b_im, -1, -2),
        lru_lam)


def _inproj_kernel(x_ref, nw_ref, w_ref, o_ref):
    xn = _rms(x_ref[...], nw_ref[...])
    o_ref[...] = _dot(xn.astype(BF16), w_ref[...])


def _inproj(x, norm_w, w_bf16, layer, tm):
    t = x.shape[0]
    return pl.pallas_call(
        _inproj_kernel,
        grid=(t // tm,),
        in_specs=[
            pl.BlockSpec((tm, D_MODEL), lambda i: (i, 0)),
            _layer_spec(norm_w, layer),
            _layer_spec(w_bf16, layer),
        ],
        out_specs=pl.BlockSpec((tm, D_IN), lambda i: (i, 0)),
        out_shape=jax.ShapeDtypeStruct((t, D_IN), F32),
        compiler_params=pltpu.CompilerParams(
            dimension_semantics=("arbitrary",), vmem_limit_bytes=VMEM_LIMIT),
        name="inproj",
    )(x, norm_w, w_bf16)


FF_CHUNK = D_FF // 2


def _ffn_kernel(final, mix_ref, x_ref, wo_ref, nw_ref, wg_ref, wu_ref, wd_ref, nf_ref, o_ref,
                hid_scr):
    x1 = x_ref[...] + _dot(mix_ref[...].astype(BF16), wo_ref[...])
    xn = _rms(x1, nw_ref[...]).astype(BF16)
    for c in range(D_FF // FF_CHUNK):
        cols = slice(c * FF_CHUNK, (c + 1) * FF_CHUNK)
        gate = _dot(xn, wg_ref[:, cols])
        up = _dot(xn, wu_ref[:, cols])
        hid_scr[:, cols] = (jax.nn.silu(gate) * up).astype(BF16)
    x2 = x1 + _dot(hid_scr[...], wd_ref[...])
    if final:
        o_ref[...] = _rms(x2, nf_ref[...])
    else:
        o_ref[...] = x2


def _ffn(mix, x, wo, norm_w, wg, wu, wd, norm_final, layer, final, tm):
    t = x.shape[0]
    row = lambda i: (i, 0)
    return pl.pallas_call(
        functools.partial(_ffn_kernel, final),
        grid=(t // tm,),
        in_specs=[
            pl.BlockSpec((tm, D_MODEL), row),
            pl.BlockSpec((tm, D_MODEL), row),
            _layer_spec(wo, layer),
            _layer_spec(norm_w, layer),
            _layer_spec(wg, layer),
            _layer_spec(wu, layer),
            _layer_spec(wd, layer),
            pl.BlockSpec((1, D_MODEL), lambda i: (0, 0)),
        ],
        out_specs=pl.BlockSpec((tm, D_MODEL), row),
        out_shape=jax.ShapeDtypeStruct((t, D_MODEL), F32),
        scratch_shapes=[pltpu.VMEM((tm, D_FF), BF16)],
        compiler_params=pltpu.CompilerParams(
            dimension_semantics=("arbitrary",), vmem_limit_bytes=VMEM_LIMIT),
        name="ffn",
    )(mix, x, wo, norm_w, wg, wu, wd, norm_final.reshape(1, D_MODEL))


def _split3(x):
    hi = x.astype(BF16)
    r1 = x - hi.astype(F32)
    mid = r1.astype(BF16)
    lo = (r1 - mid.astype(F32)).astype(BF16)
    return hi, mid, lo


def _bcast_rows(x, idxs, reps):
    width = x.shape[1]
    return jnp.concatenate(
        [jnp.broadcast_to(x[i:i + 1, :], (reps, width)) for i in idxs], axis=0)


def _lane_tiles(x):
    return [x[:, j * LANES:(j + 1) * LANES] for j in range(x.shape[1] // LANES)]


def _mixer_kernel(proj_ref, hg0_ref, s50_ref, lru0_ref, cv0_ref,
                  lb_ref, hnorm_ref, tri_ref, ones_ref, perm_ref, permt_ref,
                  wb_ref, are_ref, aim_ref, wc_ref, d_ref, wglu_ref, bglu_ref, snorm_ref,
                  cw_ref, cb_ref, wax_ref, bax_ref, lc_ref, lnorm_ref,
                  mix_ref, hgo_ref, s5o_ref, lruo_ref, cvo_ref,
                  st_hg, st_s5, st_lru, st_cv, bu_scr, la_scr, lb_scr, xp_scr, o_scr):
    step = pl.program_id(1)
    zero_head = jnp.zeros((HEAD, HEAD), F32)

    @pl.when(step == 0)
    def _():
        for u in range(SEQS):
            for p in range(PAIRS):
                st_hg[u, p] = jnp.concatenate([
                    jnp.concatenate([hg0_ref[u, 2 * p], zero_head], axis=1),
                    jnp.concatenate([zero_head, hg0_ref[u, 2 * p + 1]], axis=1)], axis=0)
            for j in range(CONV_W - 1):
                st_cv[j * SEQS + u:j * SEQS + u + 1, :] = cv0_ref[u, j:j + 1, :]
        st_s5[...] = jnp.concatenate([s50_ref[u] for u in range(SEQS)], axis=0)
        st_lru[...] = jnp.concatenate([lru0_ref[u] for u in range(SEQS)], axis=0)

    def cols(lo, width):
        return proj_ref[:, :, lo:lo + width].reshape(ROWS, width)

    tm = _dot(perm_ref[...], jnp.concatenate(_split3(cols(COL_U, D_B + 2 * D_C)), axis=0))
    u_in = tm[:, 0:D_B]
    xr = tm[:, D_B:D_B + D_C]
    gr = tm[:, D_B + D_C:D_B + 2 * D_C]
    frame_rows = lambda t: pl.ds(pl.multiple_of(t * SEQS, SEQS), SEQS)

    bu = _dot(u_in.astype(BF16), wb_ref[...])
    for j, tile in enumerate(_lane_tiles(bu)):
        bu_scr[j] = tile
    a_re = [jnp.broadcast_to(a, (SEQS, LANES)) for a in _lane_tiles(are_ref[...])]
    a_im = [jnp.broadcast_to(a, (SEQS, LANES)) for a in _lane_tiles(aim_ref[...])]

    def s5_step(t, h):
        rows = frame_rows(t)
        new_re, new_im = [], []
        for j in range(S5_TILES):
            hr, hi = h[0][j], h[1][j]
            nr = a_re[j] * hr - a_im[j] * hi + bu_scr[j, rows, :]
            ni = a_re[j] * hi + a_im[j] * hr + bu_scr[S5_TILES + j, rows, :]
            bu_scr[j, rows, :] = nr
            bu_scr[S5_TILES + j, rows, :] = ni
            new_re.append(nr)
            new_im.append(ni)
        return tuple(new_re), tuple(new_im)

    h0 = _lane_tiles(st_s5[...])
    h_re, h_im = lax.fori_loop(0, UNIT, s5_step, (tuple(h0[:S5_TILES]), tuple(h0[S5_TILES:])),
                               unroll=4)
    st_s5[...] = jnp.concatenate(list(h_re) + list(h_im), axis=1)
    h_all = jnp.concatenate([bu_scr[j] for j in range(2 * S5_TILES)], axis=1)
    y = _dot(h_all.astype(BF16), wc_ref[...]) + d_ref[...] * u_in
    z = jax.nn.gelu(y)
    z = z * jax.nn.sigmoid(_dot(z.astype(BF16), wglu_ref[...]) + bglu_ref[...])
    out_b = _rms(z, snorm_ref[...])

    tail = (CONV_W - 1) * SEQS
    xp_scr[CONV_PAD - tail:CONV_PAD, :] = st_cv[...]
    xp_scr[CONV_PAD:CONV_PAD + ROWS, :] = xr
    xc = cb_ref[...] + xr * cw_ref[CONV_W - 1:CONV_W, :]
    for j in range(CONV_W - 1):
        lo = CONV_PAD - tail + j * SEQS
        xc = xc + xp_scr[lo:lo + ROWS, :] * cw_ref[j:j + 1, :]
    st_cv[...] = xr[ROWS - tail:ROWS, :]
    gates = _dot(xc.astype(BF16), wax_ref[...]) + bax_ref[...]
    r_gate = jax.nn.sigmoid(gates[:, 0:D_C])
    i_gate = jax.nn.sigmoid(gates[:, D_C:2 * D_C])
    log_a = lc_ref[...] * r_gate
    for j, tile in enumerate(_lane_tiles(jnp.exp(log_a))):
        la_scr[j] = tile
    for j, tile in enumerate(_lane_tiles(jnp.sqrt(_neg_expm1(2.0 * log_a)) * (i_gate * xc))):
        lb_scr[j] = tile

    def lru_step(t, h):
        rows = frame_rows(t)
        new = []
        for j in range(LRU_TILES):
            nh = la_scr[j, rows, :] * h[j] + lb_scr[j, rows, :]
            lb_scr[j, rows, :] = nh
            new.append(nh)
        return tuple(new)

    h_lru = lax.fori_loop(0, UNIT, lru_step, tuple(_lane_tiles(st_lru[...])), unroll=8)
    st_lru[...] = jnp.concatenate(list(h_lru), axis=1)
    yc = jnp.concatenate([lb_scr[j] for j in range(LRU_TILES)], axis=1) * jax.nn.gelu(gr)
    out_bc = jnp.concatenate([out_b, _rms(yc, lnorm_ref[...])], axis=1)
    mix_ref[:, :, D_A:D_MODEL] = _dot(
        permt_ref[...], jnp.concatenate(_split3(out_bc), axis=0)).reshape(SEQS, UNIT, D_B + D_C)

    lbv = lb_ref[...]
    rowi = lax.broadcasted_iota(jnp.int32, (UNIT, D_A), 0)
    right0 = (rowi % (2 * SUB)) >= SUB
    right1 = rowi >= 2 * SUB
    lane_p = lax.broadcasted_iota(jnp.int32, (UNIT, PAIR_W), 1)
    head0 = lane_p < HEAD
    tq = lax.broadcasted_iota(jnp.int32, (2 * UNIT, UNIT), 0) % UNIT
    sk = lax.broadcasted_iota(jnp.int32, (2 * UNIT, UNIT), 1)
    lag = tq - sk
    mask_d = lag * (lag - tq % SUB) <= 0
    mask_0 = (tq // (2 * SUB)) == (sk // (2 * SUB))
    ki = lax.broadcasted_iota(jnp.int32, (PAIR_W, PAIR_W), 0) // HEAD
    vi = lax.broadcasted_iota(jnp.int32, (PAIR_W, PAIR_W), 1) // HEAD
    same_head = ki == vi

    def stack_heads(x):
        return jnp.concatenate(
            [jnp.where(head0, x, 0.0), jnp.where(head0, 0.0, x)], axis=0).astype(BF16)

    for u in range(SEQS):
        q = proj_ref[u, :, COL_Q:COL_Q + D_A]
        fr = proj_ref[u, :, COL_F:COL_F + D_A]
        v = proj_ref[u, :, COL_V:COL_V + D_A]
        e = jnp.exp(-jnp.abs(fr))
        rcp = 1.0 / (1.0 + e)
        pos = fr >= 0.0
        sig = jnp.where(pos, rcp, e * rcp)
        sgm = jnp.where(pos, e * rcp, rcp)
        lf = jnp.log(lbv + (1.0 - lbv) * sig)
        k = (1.0 - lbv) * sgm
        g_cum = _dot(tri_ref[...], jnp.concatenate(_split3(lf), axis=0))
        dd = g_cum - _bcast_rows(g_cum, [SUB * i + SUB // 2 - 1 for i in range(UNIT // SUB)], SUB)
        q_d = q * jnp.exp(dd)
        k_d = k * jnp.exp(-dd)
        x0 = g_cum - _bcast_rows(g_cum, [2 * SUB * i + SUB - 1 for i in range(UNIT // (2 * SUB))],
                                 2 * SUB)
        e0 = jnp.exp(jnp.where(right0, x0, -x0))
        q_0 = jnp.where(right0, q * e0, 0.0)
        k_0 = jnp.where(right0, 0.0, k * e0)
        x1 = g_cum - g_cum[2 * SUB - 1:2 * SUB, :]
        e1 = jnp.exp(jnp.where(right1, x1, -x1))
        q_1 = jnp.where(right1, q * e1, 0.0)
        k_1 = jnp.where(right1, 0.0, k * e1)
        g_end = g_cum[UNIT - 1:UNIT, :]
        q_h = q * jnp.exp(g_cum)
        k_h = k * jnp.exp(g_end - g_cum)
        dec = jnp.exp(g_end)
        for p in range(PAIRS):
            sl = slice(p * PAIR_W, (p + 1) * PAIR_W)
            att = (jnp.where(mask_d, _dot_nt(stack_heads(q_d[:, sl]), k_d[:, sl].astype(BF16)), 0.0)
                   + jnp.where(mask_0, _dot_nt(stack_heads(q_0[:, sl]), k_0[:, sl].astype(BF16)), 0.0)
                   + _dot_nt(stack_heads(q_1[:, sl]), k_1[:, sl].astype(BF16)))
            vb = v[:, sl].astype(BF16)
            both = _dot(att.astype(BF16), vb)
            st = st_hg[u, p]
            o = (jnp.where(head0, both[0:UNIT], both[UNIT:2 * UNIT])
                 + _dot(q_h[:, sl].astype(BF16), st.astype(BF16)))
            o_scr[u * UNIT:(u + 1) * UNIT, sl] = o
            upd = _dot_tn(k_h[:, sl].astype(BF16), vb)
            decay = jnp.broadcast_to(dec[:, sl], (PAIR_W, PAIR_W)).T
            st_hg[u, p] = st * decay + jnp.where(same_head, upd, 0.0)
    o = o_scr[...]
    o2 = o * o
    o2_hi = o2.astype(BF16)
    o2_lo = (o2 - o2_hi.astype(F32)).astype(BF16)
    msq = _dot(jnp.concatenate([o2_hi, o2_lo], axis=1), ones_ref[...]) * (1.0 / HEAD)
    on = o * lax.rsqrt(msq + EPS)
    mix_ref[:, :, 0:D_A] = (on * hnorm_ref[...] * jax.nn.silu(cols(COL_G, D_A))).reshape(
        SEQS, UNIT, D_A)

    @pl.when(step == pl.num_programs(1) - 1)
    def _():
        for u in range(SEQS):
            for p in range(PAIRS):
                st = st_hg[u, p]
                hgo_ref[u, 2 * p] = st[0:HEAD, 0:HEAD]
                hgo_ref[u, 2 * p + 1] = st[HEAD:PAIR_W, HEAD:PAIR_W]
            s5o_ref[u] = st_s5[u:u + 1, :]
            lruo_ref[u] = st_lru[u:u + 1, :]
            for j in range(CONV_W - 1):
                cvo_ref[u, j:j + 1, :] = st_cv[j * SEQS + u:j * SEQS + u + 1, :]


def _mixer_constants():
    t = np.arange(UNIT)
    tri = (t[None, :] <= t[:, None]).astype(np.float32)
    tri3 = np.concatenate([tri, tri, tri], axis=1)
    h = np.arange(D_A) // HEAD
    ones = (h[:, None] == h[None, :]).astype(np.float32)
    ones2 = np.concatenate([ones, ones], axis=0)
    r = np.arange(ROWS)
    src = (r % SEQS) * UNIT + r // SEQS
    perm = (src[:, None] == r[None, :]).astype(np.float32)
    perm3 = np.concatenate([perm, perm, perm], axis=1)
    permt3 = np.concatenate([perm.T, perm.T, perm.T], axis=1)
    return [jnp.asarray(c, BF16) for c in (tri3, ones2, perm3, permt3)]


def _mixer(proj, states, params, layer):
    nseq, length, _ = proj.shape
    grid = (nseq // SEQS, length // UNIT)
    constants = _mixer_constants()

    def state_spec(arr):
        nd = arr.ndim - 2
        return pl.BlockSpec((None, SEQS) + arr.shape[2:], lambda b, c: (layer, b) + (0,) * nd)

    def out_spec(shape):
        nd = len(shape) - 1
        return pl.BlockSpec((SEQS,) + shape[1:], lambda b, c: (b,) + (0,) * nd)

    def const_spec(arr):
        nd = arr.ndim
        return pl.BlockSpec(arr.shape, lambda b, c: (0,) * nd, pipeline_mode=pl.Buffered(1))

    head = ["lb", "hnorm"]
    rest = ["wb", "a_re", "a_im", "wc", "d", "wglu", "bglu", "snorm",
            "cw", "cb", "wax", "bax", "lc", "lnorm"]
    args = [params[n] for n in head] + constants + [params[n] for n in rest]
    specs = ([_layer_spec(params[n], layer) for n in head] + [const_spec(c) for c in constants]
             + [_layer_spec(params[n], layer) for n in rest])
    state_shapes = [s.shape[1:] for s in states]
    out_shape = ((jax.ShapeDtypeStruct((nseq, length, D_MODEL), F32),)
                 + tuple(jax.ShapeDtypeStruct(s, F32) for s in state_shapes))
    scratch = [
        pltpu.VMEM((SEQS, PAIRS, PAIR_W, PAIR_W), F32),
        pltpu.VMEM((SEQS, 2 * S5_W), F32),
        pltpu.VMEM((SEQS, D_C), F32),
        pltpu.VMEM(((CONV_W - 1) * SEQS, D_C), F32),
        pltpu.VMEM((2 * S5_TILES, ROWS, LANES), F32),
        pltpu.VMEM((LRU_TILES, ROWS, LANES), F32),
        pltpu.VMEM((LRU_TILES, ROWS, LANES), F32),
        pltpu.VMEM((CONV_PAD + ROWS, D_C), F32),
        pltpu.VMEM((ROWS, D_A), F32),
    ]
    return pl.pallas_call(
        _mixer_kernel,
        grid=grid,
        in_specs=([pl.BlockSpec((SEQS, UNIT, D_IN), lambda b, c: (b, c, 0))]
                  + [state_spec(s) for s in states] + specs),
        out_specs=([pl.BlockSpec((SEQS, UNIT, D_MODEL), lambda b, c: (b, c, 0))]
                   + [out_spec(s) for s in state_shapes]),
        out_shape=out_shape,
        scratch_shapes=scratch,
        compiler_params=pltpu.CompilerParams(
            dimension_semantics=("arbitrary", "arbitrary"), vmem_limit_bytes=VMEM_LIMIT),
        name="mixer",
    )(proj, *states, *args)


def _block_diag(w):
    l, h, i, j = w.shape
    return jnp.einsum("lhij,hk->lhikj", w, jnp.eye(h, dtype=w.dtype)).reshape(l, h * i, h * j)


def _mixer_params(consts, s5_c_re, s5_c_im, s5_d, s5_w_glu, s5_b_glu, s5_norm, hgrn_norm,
                  lru_conv_w, lru_conv_b, lru_wa, lru_ba, lru_wx, lru_bx, lru_norm):
    lbs, a_re, a_im, bbar_re, bbar_im, lru_c = consts
    depth = lbs.shape[0]
    eye = jnp.eye(S5_GROUPS, dtype=F32)
    row = lambda x, w: x.reshape(depth, 1, w)
    embed_b = lambda b: jnp.einsum("lgnp,gh->lgnhp", b, eye).reshape(depth, D_B, S5_W)
    embed_c = lambda c: jnp.einsum("lgnp,gh->lgphn", c, eye).reshape(depth, S5_W, D_B)
    return {
        "lb": row(lbs, D_A), "hnorm": row(hgrn_norm, D_A),
        "wb": jnp.concatenate([embed_b(bbar_re), embed_b(bbar_im)], axis=2).astype(BF16),
        "a_re": row(a_re, S5_W), "a_im": row(a_im, S5_W),
        "wc": jnp.concatenate([embed_c(s5_c_re), embed_c(-s5_c_im)], axis=1).astype(BF16),
        "d": row(s5_d, D_B), "wglu": s5_w_glu.astype(BF16), "bglu": row(s5_b_glu, D_B),
        "snorm": row(s5_norm, D_B), "cw": lru_conv_w, "cb": row(lru_conv_b, D_C),
        "wax": jnp.concatenate([_block_diag(lru_wa), _block_diag(lru_wx)], axis=2).astype(BF16),
        "bax": jnp.concatenate([lru_ba, lru_bx], axis=1).reshape(depth, 1, 2 * D_C),
        "lc": row(lru_c, D_C), "lnorm": row(lru_norm, D_C),
    }


def kernel(x_prompt, x_sample, state_hgrn, state_s5_re, state_s5_im, state_rglru, cache_conv, norm_mix, w_in, hgrn_lb, hgrn_norm, s5_lam_re, s5_lam_im, s5_log_dt, s5_b_re, s5_b_im, s5_c_re, s5_c_im, s5_d, s5_w_glu, s5_b_glu, s5_norm, lru_conv_w, lru_conv_b, lru_wa, lru_ba, lru_wx, lru_bx, lru_lam, lru_norm, w_out, norm_ffn, w_ffn_gate, w_ffn_up, w_ffn_down, norm_final):
    depth = w_in.shape[0]
    consts = _prep(hgrn_lb, s5_lam_re, s5_lam_im, s5_log_dt, s5_b_re, s5_b_im, lru_lam)
    params = _mixer_params(consts, s5_c_re, s5_c_im, s5_d, s5_w_glu, s5_b_glu, s5_norm, hgrn_norm,
                           lru_conv_w, lru_conv_b, lru_wa, lru_ba, lru_wx, lru_bx, lru_norm)
    w_in_b = w_in.astype(BF16)
    w_out_b = w_out.astype(BF16)
    wg_b = w_ffn_gate.astype(BF16)
    wu_b = w_ffn_up.astype(BF16)
    wd_b = w_ffn_down.astype(BF16)
    norm_mix3 = norm_mix.reshape(depth, 1, D_MODEL)
    norm_ffn3 = norm_ffn.reshape(depth, 1, D_MODEL)

    def trunk(x, s_hgrn, s_re, s_im, s_lru, s_conv):
        nseq, length, _ = x.shape
        xf = x.reshape(nseq * length, D_MODEL)
        states = (s_hgrn,
                  jnp.concatenate([s_re.reshape(depth, nseq, 1, S5_W),
                                   s_im.reshape(depth, nseq, 1, S5_W)], axis=-1),
                  s_lru.reshape(depth, nseq, 1, D_C), s_conv)
        outs = []
        for l in range(depth):
            proj = _inproj(xf, norm_mix3, w_in_b, l, 512).reshape(nseq, length, D_IN)
            mix, hg, s5, lru, cv = _mixer(proj, states, params, l)
            xf = _ffn(mix.reshape(nseq * length, D_MODEL), xf, w_out_b, norm_ffn3, wg_b, wu_b,
                      wd_b, norm_final, l, l == depth - 1, 512)
            outs.append((hg, s5, lru, cv))
        hg, s5, lru, cv = (jnp.stack([o[i] for o in outs]) for i in range(4))
        return (xf.reshape(nseq, length, D_MODEL), hg,
                s5[:, :, 0, 0:S5_W].reshape(depth, nseq, S5_GROUPS, S5_STATE),
                s5[:, :, 0, S5_W:].reshape(depth, nseq, S5_GROUPS, S5_STATE),
                lru[:, :, 0, :], cv)

    bp = x_prompt.shape[0]
    zeros = lambda *shape: jnp.zeros((depth, bp) + shape, F32)
    res_p = trunk(x_prompt, zeros(HEADS, HEAD, HEAD), zeros(S5_GROUPS, S5_STATE),
                  zeros(S5_GROUPS, S5_STATE), zeros(D_C), zeros(CONV_W - 1, D_C))
    res_s = trunk(x_sample, state_hgrn, state_s5_re, state_s5_im, state_rglru, cache_conv)
    return (res_p[0], res_s[0]) + res_p[1:] + res_s[1:]
```

```python
import functools

import numpy as np
import jax
import jax.numpy as jnp
from jax import lax
from jax.experimental import pallas as pl
from jax.experimental.pallas import tpu as pltpu

F32 = jnp.float32
BF16 = jnp.bfloat16

D_MODEL = 1024
D_A = 384
D_B = 256
D_C = 384
HEAD = 64
HEADS = D_A // HEAD
PAIRS = HEADS // 2
PAIR_W = 2 * HEAD
S5_GROUPS = 16
S5_GROUP = 16
S5_STATE = 64
S5_W = S5_GROUPS * S5_STATE
LRU_HEADS = 6
LRU_BLOCK = 64
CONV_W = 4
LRU_C = 8.0
D_FF = 2816
D_IN = 4 * D_A + D_B + 2 * D_C
EPS = 1e-6

LANES = 128
UNIT = 64
SEQS = 4
ROWS = UNIT * SEQS
SUB = 16
CONV_PAD = 16
PROJ_CHUNK = 256
S5_TILES = S5_W // LANES
LRU_TILES = D_C // LANES

COL_Q, COL_F, COL_V, COL_G = 0, D_A, 2 * D_A, 3 * D_A
COL_U = 4 * D_A
COL_XR = COL_U + D_B
COL_GR = COL_XR + D_C

VMEM_LIMIT = 56 * 1024 * 1024


def _dot(a, b):
    return jnp.dot(a, b, preferred_element_type=F32)


def _dot_nt(a, b):
    return lax.dot_general(a, b, (((1,), (1,)), ((), ())), preferred_element_type=F32)


def _dot_tn(a, b):
    return lax.dot_general(a, b, (((0,), (0,)), ((), ())), preferred_element_type=F32)


def _neg_expm1(x):
    u = jnp.exp(x)
    near = jnp.where(u == 1.0, -x, (1.0 - u) * x / jnp.log(u))
    return jnp.where(x < -0.5, 1.0 - u, near)


def _rms(x, w):
    return x * lax.rsqrt(jnp.mean(x * x, axis=-1, keepdims=True) + EPS) * w


def _layer_spec(arr, layer):
    nd = arr.ndim
    return pl.BlockSpec((None,) + arr.shape[1:], lambda *_: (layer,) + (0,) * (nd - 1),
                        pipeline_mode=pl.Buffered(1))


def _prep_kernel(lb_ref, lam_re_ref, lam_im_ref, log_dt_ref, bt_re_ref, bt_im_ref, lru_lam_ref,
                 lbs_ref, a_re_ref, a_im_ref, bbar_re_ref, bbar_im_ref, lru_c_ref):
    depth = lb_ref.shape[0]
    rows = [lb_ref[l:l + 1, :] for l in range(depth)]
    m = rows[0]
    for r in rows[1:]:
        m = jnp.maximum(m, r)
    es = [jnp.exp(r - m) for r in rows]
    tot = es[0]
    for e in es[1:]:
        tot = tot + e
    cs = None
    for l in range(depth):
        sm = es[l] / tot
        cs = sm if cs is None else cs + sm
        if l == 0:
            cs0 = cs
        lbs_ref[l:l + 1, :] = cs - cs0

    dt = jnp.exp(log_dt_ref[...])
    lr = jnp.minimum(lam_re_ref[...], -1e-4)
    li = lam_im_ref[...]
    mag = jnp.exp(lr * dt)
    a_re = mag * jnp.cos(li * dt)
    a_im = mag * jnp.sin(li * dt)
    den = lr * lr + li * li
    nr = a_re - 1.0
    gam_re = (nr * lr + a_im * li) / den
    gam_im = (a_im * lr - nr * li) / den
    a_re_ref[...] = a_re
    a_im_ref[...] = a_im
    b_re = bt_re_ref[...]
    b_im = bt_im_ref[...]
    bbar_re_ref[...] = gam_re * b_re - gam_im * b_im
    bbar_im_ref[...] = gam_re * b_im + gam_im * b_re

    z = -lru_lam_ref[...]
    softplus = jnp.maximum(z, 0.0) + jnp.log1p(jnp.exp(-jnp.abs(z)))
    lru_c_ref[...] = -LRU_C * softplus


def _prep(hgrn_lb, lam_re, lam_im, log_dt, b_re, b_im, lru_lam):
    depth = hgrn_lb.shape[0]
    g, p, n = S5_GROUPS, S5_STATE, S5_GROUP
    out_shape = (
        jax.ShapeDtypeStruct((depth, D_A), F32),
        jax.ShapeDtypeStruct((depth, g, 1, p), F32),
        jax.ShapeDtypeStruct((depth, g, 1, p), F32),
        jax.ShapeDtypeStruct((depth, g, n, p), F32),
        jax.ShapeDtypeStruct((depth, g, n, p), F32),
        jax.ShapeDtypeStruct((depth, D_C), F32),
    )
    return pl.pallas_call(_prep_kernel, out_shape=out_shape, name="prep")(
        hgrn_lb, lam_re.reshape(depth, g, 1, p), lam_im.reshape(depth, g, 1, p),
        log_dt.reshape(depth, g, 1, 1), jnp.swapaxes(b_re, -1, -2), jnp.swapaxes(b_im, -1, -2),
        lru_lam)


FF_CHUNK = D_FF // 2


def _ffn_kernel(final, mix_ref, x_ref, wo_ref, nw_ref, wg_ref, wu_ref, wd_ref, nf_ref, o_ref,
                hid_scr):
    x1 = x_ref[...] + _dot(mix_ref[...].astype(BF16), wo_ref[...])
    xn = _rms(x1, nw_ref[...]).astype(BF16)
    for c in range(D_FF // FF_CHUNK):
        cols = slice(c * FF_CHUNK, (c + 1) * FF_CHUNK)
        gate = _dot(xn, wg_ref[:, cols])
        up = _dot(xn, wu_ref[:, cols])
        hid_scr[:, cols] = (jax.nn.silu(gate) * up).astype(BF16)
    x2 = x1 + _dot(hid_scr[...], wd_ref[...])
    if final:
        o_ref[...] = _rms(x2, nf_ref[...])
    else:
        o_ref[...] = x2


def _ffn(mix, x, wo, norm_w, wg, wu, wd, norm_final, layer, final, tm):
    t = x.shape[0]
    row = lambda i: (i, 0)
    return pl.pallas_call(
        functools.partial(_ffn_kernel, final),
        grid=(t // tm,),
        in_specs=[
            pl.BlockSpec((tm, D_MODEL), row),
            pl.BlockSpec((tm, D_MODEL), row),
            _layer_spec(wo, layer),
            _layer_spec(norm_w, layer),
            _layer_spec(wg, layer),
            _layer_spec(wu, layer),
            _layer_spec(wd, layer),
            pl.BlockSpec((1, D_MODEL), lambda i: (0, 0)),
        ],
        out_specs=pl.BlockSpec((tm, D_MODEL), row),
        out_shape=jax.ShapeDtypeStruct((t, D_MODEL), F32),
        scratch_shapes=[pltpu.VMEM((tm, D_FF), BF16)],
        compiler_params=pltpu.CompilerParams(
            dimension_semantics=("arbitrary",), vmem_limit_bytes=VMEM_LIMIT),
        name="ffn",
    )(mix, x, wo, norm_w, wg, wu, wd, norm_final.reshape(1, D_MODEL))


def _split3(x):
    hi = x.astype(BF16)
    r1 = x - hi.astype(F32)
    mid = r1.astype(BF16)
    lo = (r1 - mid.astype(F32)).astype(BF16)
    return hi, mid, lo


def _bcast_rows(x, idxs, reps):
    width = x.shape[1]
    return jnp.concatenate(
        [jnp.broadcast_to(x[i:i + 1, :], (reps, width)) for i in idxs], axis=0)


def _lane_tiles(x):
    return [x[:, j * LANES:(j + 1) * LANES] for j in range(x.shape[1] // LANES)]


def _mixer_kernel(steps_per_seq,
                  x_ref, xnext_ref, hg0_ref, s50_ref, lru0_ref, cv0_ref, nmix_ref, win_ref,
                  lb_ref, hnorm_ref, tri_ref, ones_ref, perm_ref, permt_ref,
                  wb_ref, are_ref, aim_ref, wc_ref, d_ref, wglu_ref, bglu_ref, snorm_ref,
                  cw_ref, cb_ref, wax_ref, bax_ref, lc_ref, lnorm_ref,
                  mix_ref, hgo_ref, s5o_ref, lruo_ref, cvo_ref,
                  st_hg, st_s5, st_lru, st_cv, proj_scr, pnext_scr, bu_scr, la_scr, lb_scr, xp_scr,
                  o_scr):
    step = pl.program_id(0) % steps_per_seq

    def in_projection(x_blk):
        xn = _rms(x_blk.reshape(ROWS, D_MODEL), nmix_ref[...])
        return _dot(xn.astype(BF16), win_ref[...])

    @pl.when(pl.program_id(0) == 0)
    def _():
        pnext_scr[...] = in_projection(x_ref[...])

    proj_scr[...] = pnext_scr[...]
    zero_head = jnp.zeros((HEAD, HEAD), F32)

    @pl.when(step == 0)
    def _():
        for u in range(SEQS):
            for p in range(PAIRS):
                st_hg[u, p] = jnp.concatenate([
                    jnp.concatenate([hg0_ref[u, 2 * p], zero_head], axis=1),
                    jnp.concatenate([zero_head, hg0_ref[u, 2 * p + 1]], axis=1)], axis=0)
            for j in range(CONV_W - 1):
                st_cv[j * SEQS + u:j * SEQS + u + 1, :] = cv0_ref[u, j:j + 1, :]
        st_s5[...] = jnp.concatenate([s50_ref[u] for u in range(SEQS)], axis=0)
        st_lru[...] = jnp.concatenate([lru0_ref[u] for u in range(SEQS)], axis=0)

    def cols(lo, width):
        return proj_scr[:, lo:lo + width]

    tm = _dot(perm_ref[...], jnp.concatenate(_split3(cols(COL_U, D_B + 2 * D_C)), axis=0))
    u_in = tm[:, 0:D_B]
    xr = tm[:, D_B:D_B + D_C]
    gr = tm[:, D_B + D_C:D_B + 2 * D_C]
    frame_rows = lambda t: pl.ds(pl.multiple_of(t * SEQS, SEQS), SEQS)

    bu = _dot(u_in.astype(BF16), wb_ref[...])
    for j, tile in enumerate(_lane_tiles(bu)):
        bu_scr[j] = tile
    a_re = [jnp.broadcast_to(a, (SEQS, LANES)) for a in _lane_tiles(are_ref[...])]
    a_im = [jnp.broadcast_to(a, (SEQS, LANES)) for a in _lane_tiles(aim_ref[...])]

    def s5_step(t, h):
        rows = frame_rows(t)
        new_re, new_im = [], []
        for j in range(S5_TILES):
            hr, hi = h[0][j], h[1][j]
            nr = a_re[j] * hr - a_im[j] * hi + bu_scr[j, rows, :]
            ni = a_re[j] * hi + a_im[j] * hr + bu_scr[S5_TILES + j, rows, :]
            bu_scr[j, rows, :] = nr
            bu_scr[S5_TILES + j, rows, :] = ni
            new_re.append(nr)
            new_im.append(ni)
        return tuple(new_re), tuple(new_im)

    h0 = _lane_tiles(st_s5[...])
    h_re, h_im = lax.fori_loop(0, UNIT, s5_step, (tuple(h0[:S5_TILES]), tuple(h0[S5_TILES:])),
                               unroll=4)
    st_s5[...] = jnp.concatenate(list(h_re) + list(h_im), axis=1)

    xn_next = _rms(xnext_ref[...].reshape(ROWS, D_MODEL), nmix_ref[...]).astype(BF16)
    chunks = iter(range(D_IN // PROJ_CHUNK))

    def project_next(n):
        for _ in range(n):
            lo = next(chunks) * PROJ_CHUNK
            pnext_scr[:, lo:lo + PROJ_CHUNK] = _dot(xn_next, win_ref[:, lo:lo + PROJ_CHUNK])

    h_all = jnp.concatenate([bu_scr[j] for j in range(2 * S5_TILES)], axis=1)
    y = _dot(h_all.astype(BF16), wc_ref[...]) + d_ref[...] * u_in
    project_next(1)
    z = jax.nn.gelu(y)
    z = z * jax.nn.sigmoid(_dot(z.astype(BF16), wglu_ref[...]) + bglu_ref[...])
    project_next(1)
    out_b = _rms(z, snorm_ref[...])

    tail = (CONV_W - 1) * SEQS
    xp_scr[CONV_PAD - tail:CONV_PAD, :] = st_cv[...]
    xp_scr[CONV_PAD:CONV_PAD + ROWS, :] = xr
    xc = cb_ref[...] + xr * cw_ref[CONV_W - 1:CONV_W, :]
    for j in range(CONV_W - 1):
        lo = CONV_PAD - tail + j * SEQS
        xc = xc + xp_scr[lo:lo + ROWS, :] * cw_ref[j:j + 1, :]
    st_cv[...] = xr[ROWS - tail:ROWS, :]
    gates = _dot(xc.astype(BF16), wax_ref[...]) + bax_ref[...]
    project_next(1)
    r_gate = jax.nn.sigmoid(gates[:, 0:D_C])
    i_gate = jax.nn.sigmoid(gates[:, D_C:2 * D_C])
    log_a = lc_ref[...] * r_gate
    for j, tile in enumerate(_lane_tiles(jnp.exp(log_a))):
        la_scr[j] = tile
    project_next(1)
    for j, tile in enumerate(_lane_tiles(jnp.sqrt(_neg_expm1(2.0 * log_a)) * (i_gate * xc))):
        lb_scr[j] = tile

    def lru_step(t, h):
        rows = frame_rows(t)
        new = []
        for j in range(LRU_TILES):
            nh = la_scr[j, rows, :] * h[j] + lb_scr[j, rows, :]
            lb_scr[j, rows, :] = nh
            new.append(nh)
        return tuple(new)

    h_lru = lax.fori_loop(0, UNIT, lru_step, tuple(_lane_tiles(st_lru[...])), unroll=8)
    st_lru[...] = jnp.concatenate(list(h_lru), axis=1)
    yc = jnp.concatenate([lb_scr[j] for j in range(LRU_TILES)], axis=1) * jax.nn.gelu(gr)
    out_bc = jnp.concatenate([out_b, _rms(yc, lnorm_ref[...])], axis=1)
    mix_ref[:, :, D_A:D_MODEL] = _dot(
        permt_ref[...], jnp.concatenate(_split3(out_bc), axis=0)).reshape(SEQS, UNIT, D_B + D_C)

    units =[slice(u * UNIT, (u + 1) * UNIT) for u in range(SEQS)]
    pairs = [slice(p * PAIR_W, (p + 1) * PAIR_W) for p in range(PAIRS)]
    lbv = lb_ref[...]
    rowi = lax.broadcasted_iota(jnp.int32, (ROWS, D_A), 0) % UNIT
    right0 = (rowi % (2 * SUB)) >= SUB
    right1 = rowi >= 2 * SUB
    head0 = lax.broadcasted_iota(jnp.int32, (ROWS, D_A), 1) % PAIR_W < HEAD
    tq = lax.broadcasted_iota(jnp.int32, (UNIT, PAIR_W), 0)
    sk = lax.broadcasted_iota(jnp.int32, (UNIT, PAIR_W), 1) % HEAD
    lag = tq - sk
    mask_d = lag * (lag - tq % SUB) <= 0
    mask_0 = (tq // (2 * SUB)) == (sk // (2 * SUB))
    ki = lax.broadcasted_iota(jnp.int32, (PAIR_W, PAIR_W), 0) // HEAD
    vi = lax.broadcasted_iota(jnp.int32, (PAIR_W, PAIR_W), 1) // HEAD
    same_head = ki == vi

    def by_head(x):
        return jnp.where(head0, x, 0.0).astype(BF16), jnp.where(head0, 0.0, x).astype(BF16)

    def stacked(halves, u, p):
        return jnp.concatenate([halves[0][units[u], pairs[p]], halves[1][units[u], pairs[p]]],
                               axis=0)

    project_next(2)
    q = cols(COL_Q, D_A)
    fr = cols(COL_F, D_A)
    e = jnp.exp(-jnp.abs(fr))
    rcp = 1.0 / (1.0 + e)
    pos = fr >= 0.0
    sig = jnp.where(pos, rcp, e * rcp)
    sgm = jnp.where(pos, e * rcp, rcp)
    lf3 = _split3(jnp.log(lbv + (1.0 - lbv) * sig))
    k = (1.0 - lbv) * sgm
    g_cum = jnp.concatenate(
        [_dot(tri_ref[...], jnp.concatenate([part[rows] for part in lf3], axis=0))
         for rows in units], axis=0)
    project_next(1)
    mids = [u * UNIT + SUB * i + SUB // 2 - 1 for u in range(SEQS) for i in range(UNIT // SUB)]
    dd = g_cum - _bcast_rows(g_cum, mids, SUB)
    q_d = (q * jnp.exp(dd)).astype(BF16)
    k_d = by_head(k * jnp.exp(-dd))
    project_next(1)
    bounds = [u * UNIT + 2 * SUB * i + SUB - 1 for u in range(SEQS)
              for i in range(UNIT // (2 * SUB))]
    x0 = g_cum - _bcast_rows(g_cum, bounds, 2 * SUB)
    e0 = jnp.exp(jnp.where(right0, x0, -x0))
    q_0 = jnp.where(right0, q * e0, 0.0).astype(BF16)
    k_0 = by_head(jnp.where(right0, 0.0, k * e0))
    project_next(1)
    x1 = g_cum - _bcast_rows(g_cum, [u * UNIT + 2 * SUB - 1 for u in range(SEQS)], UNIT)
    e1 = jnp.exp(jnp.where(right1, x1, -x1))
    q_1 = jnp.where(right1, q * e1, 0.0).astype(BF16)
    k_1 = by_head(jnp.where(right1, 0.0, k * e1))
    project_next(1)
    assert next(chunks, None) is None, "every in-projection chunk must be issued"
    g_end = _bcast_rows(g_cum, [u * UNIT + UNIT - 1 for u in range(SEQS)], UNIT)
    q_h = (q * jnp.exp(g_cum)).astype(BF16)
    k_h = (k * jnp.exp(g_end - g_cum)).astype(BF16)
    v = cols(COL_V, D_A)
    vb = v.astype(BF16)
    v_heads = by_head(v)
    up = [(u, p) for u in range(SEQS) for p in range(PAIRS)]
    scores = [(_dot_nt(q_d[units[u], pairs[p]], stacked(k_d, u, p)),
               _dot_nt(q_0[units[u], pairs[p]], stacked(k_0, u, p)),
               _dot_nt(q_1[units[u], pairs[p]], stacked(k_1, u, p))) for u, p in up]
    att = [(jnp.where(mask_d, s_d, 0.0) + jnp.where(mask_0, s_0, 0.0) + s_1).astype(BF16)
           for s_d, s_0, s_1 in scores]
    states = [st_hg[u, p] for u, p in up]
    outs = [_dot(a, stacked(v_heads, u, p)) + _dot(q_h[units[u], pairs[p]], st.astype(BF16))
            for (u, p), a, st in zip(up, att, states)]
    updates = [_dot_tn(k_h[units[u], pairs[p]], vb[units[u], pairs[p]]) for u, p in up]
    for (u, p), o_up, st, upd in zip(up, outs, states, updates):
        o_scr[units[u], pairs[p]] = o_up
        dec = jnp.exp(g_cum[u * UNIT + UNIT - 1:u * UNIT + UNIT, pairs[p]])
        decay = jnp.broadcast_to(dec, (PAIR_W, PAIR_W)).T
        st_hg[u, p] = st * decay + jnp.where(same_head, upd, 0.0)
    o = o_scr[...]
    o2 = o * o
    o2_hi = o2.astype(BF16)
    o2_lo = (o2 - o2_hi.astype(F32)).astype(BF16)
    msq = _dot(jnp.concatenate([o2_hi, o2_lo], axis=1), ones_ref[...]) * (1.0 / HEAD)
    on = o * lax.rsqrt(msq + EPS)
    mix_ref[:, :, 0:D_A] = (on * hnorm_ref[...] * jax.nn.silu(cols(COL_G, D_A))).reshape(
        SEQS, UNIT, D_A)

    @pl.when(step == steps_per_seq - 1)
    def _():
        for u in range(SEQS):
            for p in range(PAIRS):
                st = st_hg[u, p]
                hgo_ref[u, 2 * p] = st[0:HEAD, 0:HEAD]
                hgo_ref[u, 2 * p + 1] = st[HEAD:PAIR_W, HEAD:PAIR_W]
            s5o_ref[u] = st_s5[u:u + 1, :]
            lruo_ref[u] = st_lru[u:u + 1, :]
            for j in range(CONV_W - 1):
                cvo_ref[u, j:j + 1, :] = st_cv[j * SEQS + u:j * SEQS + u + 1, :]


def _mixer_constants():
    t = np.arange(UNIT)
    tri = (t[None, :] <= t[:, None]).astype(np.float32)
    tri3 = np.concatenate([tri, tri, tri], axis=1)
    h = np.arange(D_A) // HEAD
    ones = (h[:, None] == h[None, :]).astype(np.float32)
    ones2 = np.concatenate([ones, ones], axis=0)
    r = np.arange(ROWS)
    src = (r % SEQS) * UNIT + r // SEQS
    perm = (src[:, None] == r[None, :]).astype(np.float32)
    perm3 = np.concatenate([perm, perm, perm], axis=1)
    permt3 = np.concatenate([perm.T, perm.T, perm.T], axis=1)
    return [jnp.asarray(c, BF16) for c in (tri3, ones2, perm3, permt3)]


def _mixer(x, states, norm_mix, w_in, params, layer):
    nseq, length, _ = x.shape
    steps_per_seq = length // UNIT
    steps = (nseq // SEQS) * steps_per_seq
    constants = _mixer_constants()

    def block_map(i):
        return (i // steps_per_seq, i % steps_per_seq, 0)

    def state_spec(arr):
        nd = arr.ndim - 2
        return pl.BlockSpec((None, SEQS) + arr.shape[2:],
                            lambda i: (layer, i // steps_per_seq) + (0,) * nd)

    def out_spec(shape):
        nd = len(shape) - 1
        return pl.BlockSpec((SEQS,) + shape[1:], lambda i: (i // steps_per_seq,) + (0,) * nd)

    def const_spec(arr):
        nd = arr.ndim
        return pl.BlockSpec(arr.shape, lambda i: (0,) * nd, pipeline_mode=pl.Buffered(1))

    head = ["lb", "hnorm"]
    rest = ["wb", "a_re", "a_im", "wc", "d", "wglu", "bglu", "snorm",
            "cw", "cb", "wax", "bax", "lc", "lnorm"]
    args = [params[n] for n in head] + constants + [params[n] for n in rest]
    specs = ([_layer_spec(params[n], layer) for n in head] + [const_spec(c) for c in constants]
             + [_layer_spec(params[n], layer) for n in rest])
    state_shapes = [s.shape[1:] for s in states]
    out_shape = ((jax.ShapeDtypeStruct((nseq, length, D_MODEL), F32),)
                 + tuple(jax.ShapeDtypeStruct(s, F32) for s in state_shapes))
    scratch = [
        pltpu.VMEM((SEQS, PAIRS, PAIR_W, PAIR_W), F32),
        pltpu.VMEM((SEQS, 2 * S5_W), F32),
        pltpu.VMEM((SEQS, D_C), F32),
        pltpu.VMEM(((CONV_W - 1) * SEQS, D_C), F32),
        pltpu.VMEM((ROWS, D_IN), F32),
        pltpu.VMEM((ROWS, D_IN), F32),
        pltpu.VMEM((2 * S5_TILES, ROWS, LANES), F32),
        pltpu.VMEM((LRU_TILES, ROWS, LANES), F32),
        pltpu.VMEM((LRU_TILES, ROWS, LANES), F32),
        pltpu.VMEM((CONV_PAD + ROWS, D_C), F32),
        pltpu.VMEM((ROWS, D_A), F32),
    ]
    return pl.pallas_call(
        functools.partial(_mixer_kernel, steps_per_seq),
        grid=(steps,),
        in_specs=([pl.BlockSpec((SEQS, UNIT, D_MODEL), block_map),
                   pl.BlockSpec((SEQS, UNIT, D_MODEL),
                                lambda i: block_map(jnp.minimum(i + 1, steps - 1)))]
                  + [state_spec(s) for s in states]
                  + [_layer_spec(norm_mix, layer), _layer_spec(w_in, layer)] + specs),
        out_specs=([pl.BlockSpec((SEQS, UNIT, D_MODEL), block_map)]
                   + [out_spec(s) for s in state_shapes]),
        out_shape=out_shape,
        scratch_shapes=scratch,
        compiler_params=pltpu.CompilerParams(
            dimension_semantics=("arbitrary",), vmem_limit_bytes=VMEM_LIMIT),
        name="mixer",
    )(x, x, *states, norm_mix, w_in, *args)


def _block_diag(w):
    l, h, i, j = w.shape
    return jnp.einsum("lhij,hk->lhikj", w, jnp.eye(h, dtype=w.dtype)).reshape(l, h * i, h * j)


def _mixer_params(consts, s5_c_re, s5_c_im, s5_d, s5_w_glu, s5_b_glu, s5_norm, hgrn_norm,
                  lru_conv_w, lru_conv_b, lru_wa, lru_ba, lru_wx, lru_bx, lru_norm):
    lbs, a_re, a_im, bbar_re, bbar_im, lru_c = consts
    depth = lbs.shape[0]
    eye = jnp.eye(S5_GROUPS, dtype=F32)
    row = lambda x, w: x.reshape(depth, 1, w)
    embed_b = lambda b: jnp.einsum("lgnp,gh->lgnhp", b, eye).reshape(depth, D_B, S5_W)
    embed_c = lambda c: jnp.einsum("lgnp,gh->lgphn", c, eye).reshape(depth, S5_W, D_B)
    return {
        "lb": row(lbs, D_A), "hnorm": row(hgrn_norm, D_A),
        "wb": jnp.concatenate([embed_b(bbar_re), embed_b(bbar_im)], axis=2).astype(BF16),
        "a_re": row(a_re, S5_W), "a_im": row(a_im, S5_W),
        "wc": jnp.concatenate([embed_c(s5_c_re), embed_c(-s5_c_im)], axis=1).astype(BF16),
        "d": row(s5_d, D_B), "wglu": s5_w_glu.astype(BF16), "bglu": row(s5_b_glu, D_B),
        "snorm": row(s5_norm, D_B), "cw": lru_conv_w, "cb": row(lru_conv_b, D_C),
        "wax": jnp.concatenate([_block_diag(lru_wa), _block_diag(lru_wx)], axis=2).astype(BF16),
        "bax": jnp.concatenate([lru_ba, lru_bx], axis=1).reshape(depth, 1, 2 * D_C),
        "lc": row(lru_c, D_C), "lnorm": row(lru_norm, D_C),
    }


def kernel(x_prompt, x_sample, state_hgrn, state_s5_re, state_s5_im, state_rglru, cache_conv, norm_mix, w_in, hgrn_lb, hgrn_norm, s5_lam_re, s5_lam_im, s5_log_dt, s5_b_re, s5_b_im, s5_c_re, s5_c_im, s5_d, s5_w_glu, s5_b_glu, s5_norm, lru_conv_w, lru_conv_b, lru_wa, lru_ba, lru_wx, lru_bx, lru_lam, lru_norm, w_out, norm_ffn, w_ffn_gate, w_ffn_up, w_ffn_down, norm_final):
    depth = w_in.shape[0]
    consts = _prep(hgrn_lb, s5_lam_re, s5_lam_im, s5_log_dt, s5_b_re, s5_b_im, lru_lam)
    params = _mixer_params(consts, s5_c_re, s5_c_im, s5_d, s5_w_glu, s5_b_glu, s5_norm, hgrn_norm,
                           lru_conv_w, lru_conv_b, lru_wa, lru_ba, lru_wx, lru_bx, lru_norm)
    w_in_b = w_in.astype(BF16)
    w_out_b = w_out.astype(BF16)
    wg_b = w_ffn_gate.astype(BF16)
    wu_b = w_ffn_up.astype(BF16)
    wd_b = w_ffn_down.astype(BF16)
    norm_mix3 = norm_mix.reshape(depth, 1, D_MODEL)
    norm_ffn3 = norm_ffn.reshape(depth, 1, D_MODEL)

    def trunk(x, s_hgrn, s_re, s_im, s_lru, s_conv):
        nseq, length, _ = x.shape
        xf = x.reshape(nseq * length, D_MODEL)
        states = (s_hgrn,
                  jnp.concatenate([s_re.reshape(depth, nseq, 1, S5_W),
                                   s_im.reshape(depth, nseq, 1, S5_W)], axis=-1),
                  s_lru.reshape(depth, nseq, 1, D_C), s_conv)
        outs = []
        for l in range(depth):
            mix, hg, s5, lru, cv = _mixer(xf.reshape(nseq, length, D_MODEL), states, norm_mix3,
                                          w_in_b, params, l)
            xf = _ffn(mix.reshape(nseq * length, D_MODEL), xf, w_out_b, norm_ffn3, wg_b, wu_b,
                      wd_b, norm_final, l, l == depth - 1, 512)
            outs.append((hg, s5, lru, cv))
        hg, s5, lru, cv = (jnp.stack([o[i] for o in outs]) for i in range(4))
        return (xf.reshape(nseq, length, D_MODEL), hg,
                s5[:, :, 0, 0:S5_W].reshape(depth, nseq, S5_GROUPS, S5_STATE),
                s5[:, :, 0, S5_W:].reshape(depth, nseq, S5_GROUPS, S5_STATE),
                lru[:, :, 0, :], cv)

    bp = x_prompt.shape[0]
    zeros = lambda *shape: jnp.zeros((depth, bp) + shape, F32)
    res_p = trunk(x_prompt, zeros(HEADS, HEAD, HEAD), zeros(S5_GROUPS, S5_STATE),
                  zeros(S5_GROUPS, S5_STATE), zeros(D_C), zeros(CONV_W - 1, D_C))
    res_s = trunk(x_sample, state_hgrn, state_s5_re, state_s5_im, state_rglru, cache_conv)
    return (res_p[0], res_s[0]) + res_p[1:] + res_s[1:]
```

```python
import functools

import numpy as np
import jax
import jax.numpy as jnp
from jax import lax
from jax.experimental import pallas as pl
from jax.experimental.pallas import tpu as pltpu

F32 = jnp.float32
BF16 = jnp.bfloat16

D_MODEL = 1024
D_A = 384
D_B = 256
D_C = 384
HEAD = 64
HEADS = D_A // HEAD
PAIRS = HEADS // 2
PAIR_W = 2 * HEAD
S5_GROUPS = 16
S5_GROUP = 16
S5_STATE = 64
S5_W = S5_GROUPS * S5_STATE
LRU_HEADS = 6
LRU_BLOCK = 64
CONV_W = 4
LRU_C = 8.0
D_FF = 2816
D_IN = 4 * D_A + D_B + 2 * D_C
EPS = 1e-6

LANES = 128
UNIT = 64
SEQS = 4
ROWS = UNIT * SEQS
SUB = 16
CONV_PAD = 16
MM_CHUNK = 256
S5_TILES = S5_W // LANES
LRU_TILES = D_C // LANES

COL_Q, COL_F, COL_V, COL_G = 0, D_A, 2 * D_A, 3 * D_A
COL_U = 4 * D_A
COL_XR = COL_U + D_B
COL_GR = COL_XR + D_C

VMEM_LIMIT = 60 * 1024 * 1024


def _dot(a, b):
    return jnp.dot(a, b, preferred_element_type=F32)


def _dot_nt(a, b):
    return lax.dot_general(a, b, (((1,), (1,)), ((), ())), preferred_element_type=F32)


def _dot_tn(a, b):
    return lax.dot_general(a, b, (((0,), (0,)), ((), ())), preferred_element_type=F32)


def _neg_expm1(x):
    u = jnp.exp(x)
    near = jnp.where(u == 1.0, -x, (1.0 - u) * x / jnp.log(u))
    return jnp.where(x < -0.5, 1.0 - u, near)


def _rms(x, w):
    return x * lax.rsqrt(jnp.mean(x * x, axis=-1, keepdims=True) + EPS) * w


def _layer_spec(arr, layer):
    nd = arr.ndim
    return pl.BlockSpec((None,) + arr.shape[1:], lambda *_: (layer,) + (0,) * (nd - 1),
                        pipeline_mode=pl.Buffered(1))


def _prep_kernel(lb_ref, lam_re_ref, lam_im_ref, log_dt_ref, bt_re_ref, bt_im_ref, lru_lam_ref,
                 lbs_ref, a_re_ref, a_im_ref, bbar_re_ref, bbar_im_ref, lru_c_ref):
    depth = lb_ref.shape[0]
    rows = [lb_ref[l:l + 1, :] for l in range(depth)]
    m = rows[0]
    for r in rows[1:]:
        m = jnp.maximum(m, r)
    es = [jnp.exp(r - m) for r in rows]
    tot = es[0]
    for e in es[1:]:
        tot = tot + e
    cs = None
    for l in range(depth):
        sm = es[l] / tot
        cs = sm if cs is None else cs + sm
        if l == 0:
            cs0 = cs
        lbs_ref[l:l + 1, :] = cs - cs0

    dt = jnp.exp(log_dt_ref[...])
    lr = jnp.minimum(lam_re_ref[...], -1e-4)
    li = lam_im_ref[...]
    mag = jnp.exp(lr * dt)
    a_re = mag * jnp.cos(li * dt)
    a_im = mag * jnp.sin(li * dt)
    den = lr * lr + li * li
    nr = a_re - 1.0
    gam_re = (nr * lr + a_im * li) / den
    gam_im = (a_im * lr - nr * li) / den
    a_re_ref[...] = a_re
    a_im_ref[...] = a_im
    b_re = bt_re_ref[...]
    b_im = bt_im_ref[...]
    bbar_re_ref[...] = gam_re * b_re - gam_im * b_im
    bbar_im_ref[...] = gam_re * b_im + gam_im * b_re

    z = -lru_lam_ref[...]
    softplus = jnp.maximum(z, 0.0) + jnp.log1p(jnp.exp(-jnp.abs(z)))
    lru_c_ref[...] = -LRU_C * softplus


def _prep(hgrn_lb, lam_re, lam_im, log_dt, b_re, b_im, lru_lam):
    depth = hgrn_lb.shape[0]
    g, p, n = S5_GROUPS, S5_STATE, S5_GROUP
    out_shape = (
        jax.ShapeDtypeStruct((depth, D_A), F32),
        jax.ShapeDtypeStruct((depth, g, 1, p), F32),
        jax.ShapeDtypeStruct((depth, g, 1, p), F32),
        jax.ShapeDtypeStruct((depth, g, n, p), F32),
        jax.ShapeDtypeStruct((depth, g, n, p), F32),
        jax.ShapeDtypeStruct((depth, D_C), F32),
    )
    return pl.pallas_call(_prep_kernel, out_shape=out_shape, name="prep")(
        hgrn_lb, lam_re.reshape(depth, g, 1, p), lam_im.reshape(depth, g, 1, p),
        log_dt.reshape(depth, g, 1, 1), jnp.swapaxes(b_re, -1, -2), jnp.swapaxes(b_im, -1, -2),
        lru_lam)


def _split3(x):
    hi = x.astype(BF16)
    r1 = x - hi.astype(F32)
    mid = r1.astype(BF16)
    lo = (r1 - mid.astype(F32)).astype(BF16)
    return hi, mid, lo


def _bcast_rows(x, idxs, reps):
    width = x.shape[1]
    return jnp.concatenate(
        [jnp.broadcast_to(x[i:i + 1, :], (reps, width)) for i in idxs], axis=0)


def _lane_tiles(x):
    return [x[:, j * LANES:(j + 1) * LANES] for j in range(x.shape[1] // LANES)]


def _layer_kernel(steps_per_seq, final,
                  x0_ref, xnext_ref, xprev_ref, hg0_ref, s50_ref, lru0_ref, cv0_ref,
                  nmix_ref, win_ref,
                  lb_ref, hnorm_ref, tri_ref, ones_ref, perm_ref, permt_ref,
                  wb_ref, are_ref, aim_ref, wc_ref, d_ref, wglu_ref, bglu_ref, snorm_ref,
                  cw_ref, cb_ref, wax_ref, bax_ref, lc_ref, lnorm_ref,
                  wo_ref, nffn_ref, wg_ref, wu_ref, wd_ref, nfin_ref,
                  out_ref, hgo_ref, s5o_ref, lruo_ref, cvo_ref,
                  st_hg, st_s5, st_lru, st_cv, proj_scr, pnext_scr, bu_scr, la_scr, lb_scr, xp_scr,
                  o_scr, mix_scr, x1_scr, xn_scr, hid_scr):
    step = pl.program_id(0) % steps_per_seq

    @pl.when(pl.program_id(0) == 0)
    def _():
        xn = _rms(x0_ref[...].reshape(ROWS, D_MODEL), nmix_ref[...])
        pnext_scr[...] = _dot(xn.astype(BF16), win_ref[...])
        mix_scr[...] = jnp.zeros((ROWS, D_MODEL), F32)

    proj_scr[...] = pnext_scr[...]
    zero_head = jnp.zeros((HEAD, HEAD), F32)

    @pl.when(step == 0)
    def _():
        for u in range(SEQS):
            for p in range(PAIRS):
                st_hg[u, p] = jnp.concatenate([
                    jnp.concatenate([hg0_ref[u, 2 * p], zero_head], axis=1),
                    jnp.concatenate([zero_head, hg0_ref[u, 2 * p + 1]], axis=1)], axis=0)
            for j in range(CONV_W - 1):
                st_cv[j * SEQS + u:j * SEQS + u + 1, :] = cv0_ref[u, j:j + 1, :]
        st_s5[...] = jnp.concatenate([s50_ref[u] for u in range(SEQS)], axis=0)
        st_lru[...] = jnp.concatenate([lru0_ref[u] for u in range(SEQS)], axis=0)

    def neighbour_stream():
        chunk = lambda c: slice(c * MM_CHUNK, (c + 1) * MM_CHUNK)
        mix_prev = mix_scr[...].astype(BF16)
        for c in range(D_MODEL // MM_CHUNK):
            x1_scr[:, chunk(c)] = (xprev_ref[:, :, chunk(c)].reshape(ROWS, MM_CHUNK)
                                   + _dot(mix_prev, wo_ref[:, chunk(c)]))
            yield
        xn_scr[...] = _rms(x1_scr[...], nffn_ref[...]).astype(BF16)
        xnext_scr_val = _rms(xnext_ref[...].reshape(ROWS, D_MODEL), nmix_ref[...]).astype(BF16)
        yield
        for c in range(D_FF // MM_CHUNK):
            xn = xn_scr[...]
            gate = _dot(xn, wg_ref[:, chunk(c)])
            up = _dot(xn, wu_ref[:, chunk(c)])
            hid_scr[:, chunk(c)] = (jax.nn.silu(gate) * up).astype(BF16)
            yield
            if c < D_IN // MM_CHUNK:
                pnext_scr[:, chunk(c)] = _dot(xnext_scr_val, win_ref[:, chunk(c)])
                yield
        for c in range(D_MODEL // MM_CHUNK):
            x2 = x1_scr[:, chunk(c)] + _dot(hid_scr[...], wd_ref[:, chunk(c)])
            if final:
                x1_scr[:, chunk(c)] = x2
            else:
                out_ref[:, :, chunk(c)] = x2.reshape(SEQS, UNIT, MM_CHUNK)
            yield
        if final:
            out_ref[...] = _rms(x1_scr[...], nfin_ref[...]).reshape(SEQS, UNIT, D_MODEL)

    stream = neighbour_stream()

    def neighbours(n):
        for _ in range(n):
            next(stream, None)

    def cols(lo, width):
        return proj_scr[:, lo:lo + width]

    tm = _dot(perm_ref[...], jnp.concatenate(_split3(cols(COL_U, D_B + 2 * D_C)), axis=0))
    neighbours(3)
    u_in = tm[:, 0:D_B]
    xr = tm[:, D_B:D_B + D_C]
    gr = tm[:, D_B + D_C:D_B + 2 * D_C]
    frame_rows = lambda t: pl.ds(pl.multiple_of(t * SEQS, SEQS), SEQS)

    bu = _dot(u_in.astype(BF16), wb_ref[...])
    for j, tile in enumerate(_lane_tiles(bu)):
        bu_scr[j] = tile
    neighbours(2)
    a_re = [jnp.broadcast_to(a, (SEQS, LANES)) for a in _lane_tiles(are_ref[...])]
    a_im = [jnp.broadcast_to(a, (SEQS, LANES)) for a in _lane_tiles(aim_ref[...])]

    def s5_step(t, h):
        rows = frame_rows(t)
        new_re, new_im = [], []
        for j in range(S5_TILES):
            hr, hi = h[0][j], h[1][j]
            nr = a_re[j] * hr - a_im[j] * hi + bu_scr[j, rows, :]
            ni = a_re[j] * hi + a_im[j] * hr + bu_scr[S5_TILES + j, rows, :]
            bu_scr[j, rows, :] = nr
            bu_scr[S5_TILES + j, rows, :] = ni
            new_re.append(nr)
            new_im.append(ni)
        return tuple(new_re), tuple(new_im)

    h0 = _lane_tiles(st_s5[...])
    h_re, h_im = lax.fori_loop(0, UNIT, s5_step, (tuple(h0[:S5_TILES]), tuple(h0[S5_TILES:])),
                               unroll=4)
    st_s5[...] = jnp.concatenate(list(h_re) + list(h_im), axis=1)

    h_all = jnp.concatenate([bu_scr[j] for j in range(2 * S5_TILES)], axis=1)
    y = _dot(h_all.astype(BF16), wc_ref[...]) + d_ref[...] * u_in
    neighbours(2)
    z = jax.nn.gelu(y)
    z = z * jax.nn.sigmoid(_dot(z.astype(BF16), wglu_ref[...]) + bglu_ref[...])
    neighbours(2)
    out_b = _rms(z, snorm_ref[...])

    tail = (CONV_W - 1) * SEQS
    xp_scr[CONV_PAD - tail:CONV_PAD, :] = st_cv[...]
    xp_scr[CONV_PAD:CONV_PAD + ROWS, :] = xr
    xc = cb_ref[...] + xr * cw_ref[CONV_W - 1:CONV_W, :]
    for j in range(CONV_W - 1):
        lo = CONV_PAD - tail + j * SEQS
        xc = xc + xp_scr[lo:lo + ROWS, :] * cw_ref[j:j + 1, :]
    st_cv[...] = xr[ROWS - tail:ROWS, :]
    gates = _dot(xc.astype(BF16), wax_ref[...]) + bax_ref[...]
    neighbours(2)
    r_gate = jax.nn.sigmoid(gates[:, 0:D_C])
    i_gate = jax.nn.sigmoid(gates[:, D_C:2 * D_C])
    log_a = lc_ref[...] * r_gate
    for j, tile in enumerate(_lane_tiles(jnp.exp(log_a))):
        la_scr[j] = tile
    neighbours(2)
    for j, tile in enumerate(_lane_tiles(jnp.sqrt(_neg_expm1(2.0 * log_a)) * (i_gate * xc))):
        lb_scr[j] = tile

    def lru_step(t, h):
        rows = frame_rows(t)
        new = []
        for j in range(LRU_TILES):
            nh = la_scr[j, rows, :] * h[j] + lb_scr[j, rows, :]
            lb_scr[j, rows, :] = nh
            new.append(nh)
        return tuple(new)

    h_lru = lax.fori_loop(0, UNIT, lru_step, tuple(_lane_tiles(st_lru[...])), unroll=8)
    st_lru[...] = jnp.concatenate(list(h_lru), axis=1)
    yc = jnp.concatenate([lb_scr[j] for j in range(LRU_TILES)], axis=1) * jax.nn.gelu(gr)
    out_bc = jnp.concatenate([out_b, _rms(yc, lnorm_ref[...])], axis=1)
    mix_scr[:, D_A:D_MODEL] = _dot(permt_ref[...], jnp.concatenate(_split3(out_bc), axis=0))

    units = [slice(u * UNIT, (u + 1) * UNIT) for u in range(SEQS)]
    pairs = [slice(p * PAIR_W, (p + 1) * PAIR_W) for p in range(PAIRS)]
    lbv = lb_ref[...]
    rowi = lax.broadcasted_iota(jnp.int32, (ROWS, D_A), 0) % UNIT
    right0 = (rowi % (2 * SUB)) >= SUB
    right1 = rowi >= 2 * SUB
    head0 = lax.broadcasted_iota(jnp.int32, (ROWS, D_A), 1) % PAIR_W < HEAD
    tq = lax.broadcasted_iota(jnp.int32, (UNIT, PAIR_W), 0)
    sk = lax.broadcasted_iota(jnp.int32, (UNIT, PAIR_W), 1) % HEAD
    lag = tq - sk
    mask_d = lag * (lag - tq % SUB) <= 0
    mask_0 = (tq // (2 * SUB)) == (sk // (2 * SUB))
    ki = lax.broadcasted_iota(jnp.int32, (PAIR_W, PAIR_W), 0) // HEAD
    vi = lax.broadcasted_iota(jnp.int32, (PAIR_W, PAIR_W), 1) // HEAD
    same_head = ki == vi

    def by_head(x):
        return jnp.where(head0, x, 0.0).astype(BF16), jnp.where(head0, 0.0, x).astype(BF16)

    def stacked(halves, u, p):
        return jnp.concatenate([halves[0][units[u], pairs[p]], halves[1][units[u], pairs[p]]],
                               axis=0)

    neighbours(3)
    q = cols(COL_Q, D_A)
    fr = cols(COL_F, D_A)
    e = jnp.exp(-jnp.abs(fr))
    rcp = 1.0 / (1.0 + e)
    pos = fr >= 0.0
    sig = jnp.where(pos, rcp, e * rcp)
    sgm = jnp.where(pos, e * rcp, rcp)
    lf3 = _split3(jnp.log(lbv + (1.0 - lbv) * sig))
    k = (1.0 - lbv) * sgm
    g_cum = jnp.concatenate(
        [_dot(tri_ref[...], jnp.concatenate([part[rows] for part in lf3], axis=0))
         for rows in units], axis=0)
    neighbours(2)
    mids = [u * UNIT + SUB * i + SUB // 2 - 1 for u in range(SEQS) for i in range(UNIT // SUB)]
    dd = g_cum - _bcast_rows(g_cum, mids, SUB)
    q_d = (q * jnp.exp(dd)).astype(BF16)
    k_d = by_head(k * jnp.exp(-dd))
    neighbours(2)
    bounds = [u * UNIT + 2 * SUB * i + SUB - 1 for u in range(SEQS)
              for i in range(UNIT // (2 * SUB))]
    x0 = g_cum - _bcast_rows(g_cum, bounds, 2 * SUB)
    e0 = jnp.exp(jnp.where(right0, x0, -x0))
    q_0 = jnp.where(right0, q * e0, 0.0).astype(BF16)
    k_0 = by_head(jnp.where(right0, 0.0, k * e0))
    neighbours(2)
    x1 = g_cum - _bcast_rows(g_cum, [u * UNIT + 2 * SUB - 1 for u in range(SEQS)], UNIT)
    e1 = jnp.exp(jnp.where(right1, x1, -x1))
    q_1 = jnp.where(right1, q * e1, 0.0).astype(BF16)
    k_1 = by_head(jnp.where(right1, 0.0, k * e1))
    neighbours(2)
    g_end = _bcast_rows(g_cum, [u * UNIT + UNIT - 1 for u in range(SEQS)], UNIT)
    q_h = (q * jnp.exp(g_cum)).astype(BF16)
    k_h = (k * jnp.exp(g_end - g_cum)).astype(BF16)
    v = cols(COL_V, D_A)
    vb = v.astype(BF16)
    v_heads = by_head(v)
    neighbours(2)
    up = [(u, p) for u in range(SEQS) for p in range(PAIRS)]
    scores = [(_dot_nt(q_d[units[u], pairs[p]], stacked(k_d, u, p)),
               _dot_nt(q_0[units[u], pairs[p]], stacked(k_0, u, p)),
               _dot_nt(q_1[units[u], pairs[p]], stacked(k_1, u, p))) for u, p in up]
    att = [(jnp.where(mask_d, s_d, 0.0) + jnp.where(mask_0, s_0, 0.0) + s_1).astype(BF16)
           for s_d, s_0, s_1 in scores]
    neighbours(2)
    states = [st_hg[u, p] for u, p in up]
    outs = [_dot(a, stacked(v_heads, u, p)) + _dot(q_h[units[u], pairs[p]], st.astype(BF16))
            for (u, p), a, st in zip(up, att, states)]
    updates = [_dot_tn(k_h[units[u], pairs[p]], vb[units[u], pairs[p]]) for u, p in up]
    for (u, p), o_up, st, upd in zip(up, outs, states, updates):
        o_scr[units[u], pairs[p]] = o_up
        dec = jnp.exp(g_cum[u * UNIT + UNIT - 1:u * UNIT + UNIT, pairs[p]])
        decay = jnp.broadcast_to(dec, (PAIR_W, PAIR_W)).T
        st_hg[u, p] = st * decay + jnp.where(same_head, upd, 0.0)
    neighbours(2)
    o = o_scr[...]
    o2 = o * o
    o2_hi = o2.astype(BF16)
    o2_lo = (o2 - o2_hi.astype(F32)).astype(BF16)
    msq = _dot(jnp.concatenate([o2_hi, o2_lo], axis=1), ones_ref[...]) * (1.0 / HEAD)
    on = o * lax.rsqrt(msq + EPS)
    mix_scr[:, 0:D_A] = on * hnorm_ref[...] * jax.nn.silu(cols(COL_G, D_A))
    for _ in stream:
        pass

    @pl.when(step == steps_per_seq - 1)
    def _():
        for u in range(SEQS):
            for p in range(PAIRS):
                st = st_hg[u, p]
                hgo_ref[u, 2 * p] = st[0:HEAD, 0:HEAD]
                hgo_ref[u, 2 * p + 1] = st[HEAD:PAIR_W, HEAD:PAIR_W]
            s5o_ref[u] = st_s5[u:u + 1, :]
            lruo_ref[u] = st_lru[u:u + 1, :]
            for j in range(CONV_W - 1):
                cvo_ref[u, j:j + 1, :] = st_cv[j * SEQS + u:j * SEQS + u + 1, :]


def _layer_constants():
    t = np.arange(UNIT)
    tri = (t[None, :] <= t[:, None]).astype(np.float32)
    tri3 = np.concatenate([tri, tri, tri], axis=1)
    h = np.arange(D_A) // HEAD
    ones = (h[:, None] == h[None, :]).astype(np.float32)
    ones2 = np.concatenate([ones, ones], axis=0)
    r = np.arange(ROWS)
    src = (r % SEQS) * UNIT + r // SEQS
    perm = (src[:, None] == r[None, :]).astype(np.float32)
    perm3 = np.concatenate([perm, perm, perm], axis=1)
    permt3 = np.concatenate([perm.T, perm.T, perm.T], axis=1)
    return [jnp.asarray(c, BF16) for c in (tri3, ones2, perm3, permt3)]


def _layer(x, states, weights, params, norm_final, layer, final):
    nseq, length, _ = x.shape
    steps_per_seq = length // UNIT
    steps = (nseq // SEQS) * steps_per_seq
    constants = _layer_constants()

    def block_map(i):
        return (i // steps_per_seq, i % steps_per_seq, 0)

    def group(i):
        return jnp.minimum(i, steps - 1) // steps_per_seq

    def state_spec(arr):
        nd = arr.ndim - 2
        return pl.BlockSpec((None, SEQS) + arr.shape[2:], lambda i: (layer, group(i)) + (0,) * nd)

    def out_spec(shape):
        nd = len(shape) - 1
        return pl.BlockSpec((SEQS,) + shape[1:], lambda i: (group(i),) + (0,) * nd)

    def const_spec(arr):
        nd = arr.ndim
        return pl.BlockSpec(arr.shape, lambda i: (0,) * nd, pipeline_mode=pl.Buffered(1))

    block = (SEQS, UNIT, D_MODEL)
    head = ["lb", "hnorm"]
    rest = ["wb", "a_re", "a_im", "wc", "d", "wglu", "bglu", "snorm",
            "cw", "cb", "wax", "bax", "lc", "lnorm"]
    ffn = ["w_out", "norm_ffn", "w_gate", "w_up", "w_down"]
    args = ([x, x, x] + list(states) + [weights["norm_mix"], weights["w_in"]]
            + [params[n] for n in head] + constants + [params[n] for n in rest]
            + [weights[n] for n in ffn] + [norm_final])
    in_specs = ([pl.BlockSpec(block, lambda i: (0, 0, 0), pipeline_mode=pl.Buffered(1)),
                 pl.BlockSpec(block, lambda i: block_map(jnp.minimum(i + 1, steps - 1))),
                 pl.BlockSpec(block, lambda i: block_map(jnp.maximum(i - 1, 0)))]
                + [state_spec(s) for s in states]
                + [_layer_spec(weights["norm_mix"], layer), _layer_spec(weights["w_in"], layer)]
                + [_layer_spec(params[n], layer) for n in head]
                + [const_spec(c) for c in constants]
                + [_layer_spec(params[n], layer) for n in rest]
                + [_layer_spec(weights[n], layer) for n in ffn]
                + [const_spec(norm_final)])
    state_shapes = [s.shape[1:] for s in states]
    out_shape = ((jax.ShapeDtypeStruct((nseq, length, D_MODEL), F32),)
                 + tuple(jax.ShapeDtypeStruct(s, F32) for s in state_shapes))
    scratch = [
        pltpu.VMEM((SEQS, PAIRS, PAIR_W, PAIR_W), F32),
        pltpu.VMEM((SEQS, 2 * S5_W), F32),
        pltpu.VMEM((SEQS, D_C), F32),
        pltpu.VMEM(((CONV_W - 1) * SEQS, D_C), F32),
        pltpu.VMEM((ROWS, D_IN), F32),
        pltpu.VMEM((ROWS, D_IN), F32),
        pltpu.VMEM((2 * S5_TILES, ROWS, LANES), F32),
        pltpu.VMEM((LRU_TILES, ROWS, LANES), F32),
        pltpu.VMEM((LRU_TILES, ROWS, LANES), F32),
        pltpu.VMEM((CONV_PAD + ROWS, D_C), F32),
        pltpu.VMEM((ROWS, D_A), F32),
        pltpu.VMEM((ROWS, D_MODEL), F32),
        pltpu.VMEM((ROWS, D_MODEL), F32),
        pltpu.VMEM((ROWS, D_MODEL), BF16),
        pltpu.VMEM((ROWS, D_FF), BF16),
    ]
    return pl.pallas_call(
        functools.partial(_layer_kernel, steps_per_seq, final),
        grid=(steps + 1,),
        in_specs=in_specs,
        out_specs=([pl.BlockSpec(block, lambda i: block_map(jnp.maximum(i - 1, 0)))]
                   + [out_spec(s) for s in state_shapes]),
        out_shape=out_shape,
        scratch_shapes=scratch,
        compiler_params=pltpu.CompilerParams(
            dimension_semantics=("arbitrary",), vmem_limit_bytes=VMEM_LIMIT),
        name="layer",
    )(*args)


def _block_diag(w):
    l, h, i, j = w.shape
    return jnp.einsum("lhij,hk->lhikj", w, jnp.eye(h, dtype=w.dtype)).reshape(l, h * i, h * j)


def _mixer_params(consts, s5_c_re, s5_c_im, s5_d, s5_w_glu, s5_b_glu, s5_norm, hgrn_norm,
                  lru_conv_w, lru_conv_b, lru_wa, lru_ba, lru_wx, lru_bx, lru_norm):
    lbs, a_re, a_im, bbar_re, bbar_im, lru_c = consts
    depth = lbs.shape[0]
    eye = jnp.eye(S5_GROUPS, dtype=F32)
    row = lambda x, w: x.reshape(depth, 1, w)
    embed_b = lambda b: jnp.einsum("lgnp,gh->lgnhp", b, eye).reshape(depth, D_B, S5_W)
    embed_c = lambda c: jnp.einsum("lgnp,gh->lgphn", c, eye).reshape(depth, S5_W, D_B)
    return {
        "lb": row(lbs, D_A), "hnorm": row(hgrn_norm, D_A),
        "wb": jnp.concatenate([embed_b(bbar_re), embed_b(bbar_im)], axis=2).astype(BF16),
        "a_re": row(a_re, S5_W), "a_im": row(a_im, S5_W),
        "wc": jnp.concatenate([embed_c(s5_c_re), embed_c(-s5_c_im)], axis=1).astype(BF16),
        "d": row(s5_d, D_B), "wglu": s5_w_glu.astype(BF16), "bglu": row(s5_b_glu, D_B),
        "snorm": row(s5_norm, D_B), "cw": lru_conv_w, "cb": row(lru_conv_b, D_C),
        "wax": jnp.concatenate([_block_diag(lru_wa), _block_diag(lru_wx)], axis=2).astype(BF16),
        "bax": jnp.concatenate([lru_ba, lru_bx], axis=1).reshape(depth, 1, 2 * D_C),
        "lc": row(lru_c, D_C), "lnorm": row(lru_norm, D_C),
    }


def kernel(x_prompt, x_sample, state_hgrn, state_s5_re, state_s5_im, state_rglru, cache_conv, norm_mix, w_in, hgrn_lb, hgrn_norm, s5_lam_re, s5_lam_im, s5_log_dt, s5_b_re, s5_b_im, s5_c_re, s5_c_im, s5_d, s5_w_glu, s5_b_glu, s5_norm, lru_conv_w, lru_conv_b, lru_wa, lru_ba, lru_wx, lru_bx, lru_lam, lru_norm, w_out, norm_ffn, w_ffn_gate, w_ffn_up, w_ffn_down, norm_final):
    depth = w_in.shape[0]
    consts = _prep(hgrn_lb, s5_lam_re, s5_lam_im, s5_log_dt, s5_b_re, s5_b_im, lru_lam)
    params = _mixer_params(consts, s5_c_re, s5_c_im, s5_d, s5_w_glu, s5_b_glu, s5_norm, hgrn_norm,
                           lru_conv_w, lru_conv_b, lru_wa, lru_ba, lru_wx, lru_bx, lru_norm)
    weights = {
        "norm_mix": norm_mix.reshape(depth, 1, D_MODEL), "w_in": w_in.astype(BF16),
        "w_out": w_out.astype(BF16), "norm_ffn": norm_ffn.reshape(depth, 1, D_MODEL),
        "w_gate": w_ffn_gate.astype(BF16), "w_up": w_ffn_up.astype(BF16),
        "w_down": w_ffn_down.astype(BF16),
    }
    norm_final2 = norm_final.reshape(1, D_MODEL)

    def trunk(x, s_hgrn, s_re, s_im, s_lru, s_conv):
        nseq = x.shape[0]
        states = (s_hgrn,
                  jnp.concatenate([s_re.reshape(depth, nseq, 1, S5_W),
                                   s_im.reshape(depth, nseq, 1, S5_W)], axis=-1),
                  s_lru.reshape(depth, nseq, 1, D_C), s_conv)
        outs = []
        for l in range(depth):
            x, hg, s5, lru, cv = _layer(x, states, weights, params, norm_final2, l, l == depth - 1)
            outs.append((hg, s5, lru, cv))
        hg, s5, lru, cv = (jnp.stack([o[i] for o in outs]) for i in range(4))
        return (x, hg,
                s5[:, :, 0, 0:S5_W].reshape(depth, nseq, S5_GROUPS, S5_STATE),
                s5[:, :, 0, S5_W:].reshape(depth, nseq, S5_GROUPS, S5_STATE),
                lru[:, :, 0, :], cv)

    bp = x_prompt.shape[0]
    zeros = lambda *shape: jnp.zeros((depth, bp) + shape, F32)
    res_p = trunk(x_prompt, zeros(HEADS, HEAD, HEAD), zeros(S5_GROUPS, S5_STATE),
                  zeros(S5_GROUPS, S5_STATE), zeros(D_C), zeros(CONV_W - 1, D_C))
    res_s = trunk(x_sample, state_hgrn, state_s5_re, state_s5_im, state_rglru, cache_conv)
    return (res_p[0], res_s[0]) + res_p[1:] + res_s[1:]
```

```python
import functools

import numpy as np
import jax
import jax.numpy as jnp
from jax import lax
from jax.experimental import pallas as pl
from jax.experimental.pallas import tpu as pltpu

F32 = jnp.float32
BF16 = jnp.bfloat16

D_MODEL = 1024
D_A = 384
D_B = 256
D_C = 384
HEAD = 64
HEADS = D_A // HEAD
PAIRS = HEADS // 2
PAIR_W = 2 * HEAD
S5_GROUPS = 16
S5_GROUP = 16
S5_STATE = 64
S5_W = S5_GROUPS * S5_STATE
LRU_HEADS = 6
LRU_BLOCK = 64
CONV_W = 4
LRU_C = 8.0
D_FF = 2816
D_IN = 4 * D_A + D_B + 2 * D_C
EPS = 1e-6

LANES = 128
UNIT = 64
SEQS = 4
ROWS = UNIT * SEQS
SUB = 16
CONV_PAD = 16
MM_CHUNK = 256
S5_TILES = S5_W // LANES
LRU_TILES = D_C // LANES

COL_Q, COL_F, COL_V, COL_G = 0, D_A, 2 * D_A, 3 * D_A
COL_U = 4 * D_A
COL_XR = COL_U + D_B
COL_GR = COL_XR + D_C

VMEM_LIMIT = 60 * 1024 * 1024


def _dot(a, b):
    return jnp.dot(a, b, preferred_element_type=F32)


def _dot_nt(a, b):
    return lax.dot_general(a, b, (((1,), (1,)), ((), ())), preferred_element_type=F32)


def _dot_tn(a, b):
    return lax.dot_general(a, b, (((0,), (0,)), ((), ())), preferred_element_type=F32)


def _neg_expm1(x):
    u = jnp.exp(x)
    near = jnp.where(u == 1.0, -x, (1.0 - u) * x / jnp.log(u))
    return jnp.where(x < -0.5, 1.0 - u, near)


def _rms(x, w):
    return x * lax.rsqrt(jnp.mean(x * x, axis=-1, keepdims=True) + EPS) * w


def _layer_spec(arr, layer):
    nd = arr.ndim
    return pl.BlockSpec((None,) + arr.shape[1:], lambda *_: (layer,) + (0,) * (nd - 1),
                        pipeline_mode=pl.Buffered(1))


def _prep_kernel(lb_ref, lam_re_ref, lam_im_ref, log_dt_ref, bt_re_ref, bt_im_ref, lru_lam_ref,
                 lbs_ref, a_re_ref, a_im_ref, bbar_re_ref, bbar_im_ref, lru_c_ref):
    depth = lb_ref.shape[0]
    rows = [lb_ref[l:l + 1, :] for l in range(depth)]
    m = rows[0]
    for r in rows[1:]:
        m = jnp.maximum(m, r)
    es = [jnp.exp(r - m) for r in rows]
    tot = es[0]
    for e in es[1:]:
        tot = tot + e
    cs = None
    for l in range(depth):
        sm = es[l] / tot
        cs = sm if cs is None else cs + sm
        if l == 0:
            cs0 = cs
        lbs_ref[l:l + 1, :] = cs - cs0

    dt = jnp.exp(log_dt_ref[...])
    lr = jnp.minimum(lam_re_ref[...], -1e-4)
    li = lam_im_ref[...]
    mag = jnp.exp(lr * dt)
    a_re = mag * jnp.cos(li * dt)
    a_im = mag * jnp.sin(li * dt)
    den = lr * lr + li * li
    nr = a_re - 1.0
    gam_re = (nr * lr + a_im * li) / den
    gam_im = (a_im * lr - nr * li) / den
    a_re_ref[...] = a_re
    a_im_ref[...] = a_im
    b_re = bt_re_ref[...]
    b_im = bt_im_ref[...]
    bbar_re_ref[...] = gam_re * b_re - gam_im * b_im
    bbar_im_ref[...] = gam_re * b_im + gam_im * b_re

    z = -lru_lam_ref[...]
    softplus = jnp.maximum(z, 0.0) + jnp.log1p(jnp.exp(-jnp.abs(z)))
    lru_c_ref[...] = -LRU_C * softplus


def _prep(hgrn_lb, lam_re, lam_im, log_dt, b_re, b_im, lru_lam):
    depth = hgrn_lb.shape[0]
    g, p, n = S5_GROUPS, S5_STATE, S5_GROUP
    out_shape = (
        jax.ShapeDtypeStruct((depth, D_A), F32),
        jax.ShapeDtypeStruct((depth, g, 1, p), F32),
        jax.ShapeDtypeStruct((depth, g, 1, p), F32),
        jax.ShapeDtypeStruct((depth, g, n, p), F32),
        jax.ShapeDtypeStruct((depth, g, n, p), F32),
        jax.ShapeDtypeStruct((depth, D_C), F32),
    )
    return pl.pallas_call(_prep_kernel, out_shape=out_shape, name="prep")(
        hgrn_lb, lam_re.reshape(depth, g, 1, p), lam_im.reshape(depth, g, 1, p),
        log_dt.reshape(depth, g, 1, 1), jnp.swapaxes(b_re, -1, -2), jnp.swapaxes(b_im, -1, -2),
        lru_lam)


def _split3(x):
    hi = x.astype(BF16)
    r1 = x - hi.astype(F32)
    mid = r1.astype(BF16)
    lo = (r1 - mid.astype(F32)).astype(BF16)
    return hi, mid, lo


def _bcast_rows(x, idxs, reps):
    width = x.shape[1]
    return jnp.concatenate(
        [jnp.broadcast_to(x[i:i + 1, :], (reps, width)) for i in idxs], axis=0)


def _lane_tiles(x):
    return [x[:, j * LANES:(j + 1) * LANES] for j in range(x.shape[1] // LANES)]


def _layer_kernel(steps_per_seq, final,
                  x0_ref, xnext_ref, xprev_ref, hg0_ref, s50_ref, lru0_ref, cv0_ref,
                  nmix_ref, win_ref,
                  lb_ref, hnorm_ref, tri_ref, ones_ref, perm_ref, permt_ref,
                  wb_ref, are_ref, aim_ref, wc_ref, d_ref, wglu_ref, bglu_ref, snorm_ref,
                  cw_ref, cb_ref, wax_ref, bax_ref, lc_ref, lnorm_ref,
                  wo_ref, nffn_ref, wg_ref, wu_ref, wd_ref, nfin_ref,
                  out_ref, hgo_ref, s5o_ref, lruo_ref, cvo_ref,
                  st_hg, st_s5, st_lru, st_cv, proj_scr, pnext_scr, bu_scr, la_scr, lb_scr, xp_scr,
                  o_scr, mix_scr, x1_scr, xn_scr, hid_scr):
    step = pl.program_id(0) % steps_per_seq

    @pl.when(pl.program_id(0) == 0)
    def _():
        xn = _rms(x0_ref[...].reshape(ROWS, D_MODEL), nmix_ref[...])
        pnext_scr[...] = _dot(xn.astype(BF16), win_ref[...])
        mix_scr[...] = jnp.zeros((ROWS, D_MODEL), F32)

    zero_head = jnp.zeros((HEAD, HEAD), F32)

    @pl.when(step == 0)
    def _():
        for u in range(SEQS):
            for p in range(PAIRS):
                st_hg[u, p] = jnp.concatenate([
                    jnp.concatenate([hg0_ref[u, 2 * p], zero_head], axis=1),
                    jnp.concatenate([zero_head, hg0_ref[u, 2 * p + 1]], axis=1)], axis=0)
            for j in range(CONV_W - 1):
                st_cv[j * SEQS + u:j * SEQS + u + 1, :] = cv0_ref[u, j:j + 1, :]
        st_s5[...] = jnp.concatenate([s50_ref[u] for u in range(SEQS)], axis=0)
        st_lru[...] = jnp.concatenate([lru0_ref[u] for u in range(SEQS)], axis=0)

    def neighbour_stream():
        chunk = lambda c: slice(c * MM_CHUNK, (c + 1) * MM_CHUNK)
        mix_prev = mix_scr[...].astype(BF16)
        for c in range(D_MODEL // MM_CHUNK):
            x1_scr[:, chunk(c)] = (xprev_ref[:, :, chunk(c)].reshape(ROWS, MM_CHUNK)
                                   + _dot(mix_prev, wo_ref[:, chunk(c)]))
            yield
        xn_scr[...] = _rms(x1_scr[...], nffn_ref[...]).astype(BF16)
        xnext_scr_val = _rms(xnext_ref[...].reshape(ROWS, D_MODEL), nmix_ref[...]).astype(BF16)
        yield
        for c in range(D_FF // MM_CHUNK):
            xn = xn_scr[...]
            gate = _dot(xn, wg_ref[:, chunk(c)])
            yield
            up = _dot(xn, wu_ref[:, chunk(c)])
            hid_scr[:, chunk(c)] = (jax.nn.silu(gate) * up).astype(BF16)
            yield
            if c < D_IN // MM_CHUNK:
                pnext_scr[:, chunk(c)] = _dot(xnext_scr_val, win_ref[:, chunk(c)])
                yield
        for c in range(D_MODEL // MM_CHUNK):
            x2 = x1_scr[:, chunk(c)] + _dot(hid_scr[...], wd_ref[:, chunk(c)])
            if final:
                x1_scr[:, chunk(c)] = x2
            else:
                out_ref[:, :, chunk(c)] = x2.reshape(SEQS, UNIT, MM_CHUNK)
            yield
        if final:
            out_ref[...] = _rms(x1_scr[...], nfin_ref[...]).reshape(SEQS, UNIT, D_MODEL)

    stream = neighbour_stream()

    def neighbours(n):
        for _ in range(n):
            next(stream, None)

    def cols(lo, width):
        return proj_scr[:, lo:lo + width]

    neighbours(2)
    proj_scr[:, 0:COL_U] = pnext_scr[:, 0:COL_U]
    proj_scr[:, COL_GR:D_IN] = pnext_scr[:, COL_GR:D_IN]
    tm_parts = jnp.concatenate(_split3(pnext_scr[:, COL_U:COL_GR]), axis=0)
    neighbours(1)
    tm = _dot(perm_ref[...], tm_parts)
    neighbours(1)
    u_in = tm[:, 0:D_B]
    xr = tm[:, D_B:D_B + D_C]
    frame_rows = lambda t: slice(t * SEQS, (t + 1) * SEQS)

    def scan(step_fn, carry, gap):
        for t in range(UNIT):
            carry = step_fn(t, carry)
            if t % gap == gap - 1:
                neighbours(1)
        return carry

    bu = _dot(u_in.astype(BF16), wb_ref[...])
    neighbours(1)
    for j, tile in enumerate(_lane_tiles(bu)):
        bu_scr[j] = tile
    neighbours(2)
    a_re = [jnp.broadcast_to(a, (SEQS, LANES)) for a in _lane_tiles(are_ref[...])]
    a_im = [jnp.broadcast_to(a, (SEQS, LANES)) for a in _lane_tiles(aim_ref[...])]

    def s5_step(t, h):
        rows = frame_rows(t)
        new_re, new_im = [], []
        for j in range(S5_TILES):
            hr, hi = h[0][j], h[1][j]
            nr = a_re[j] * hr - a_im[j] * hi + bu_scr[j, rows, :]
            ni = a_re[j] * hi + a_im[j] * hr + bu_scr[S5_TILES + j, rows, :]
            bu_scr[j, rows, :] = nr
            bu_scr[S5_TILES + j, rows, :] = ni
            new_re.append(nr)
            new_im.append(ni)
        return tuple(new_re), tuple(new_im)

    h0 = _lane_tiles(st_s5[...])
    h_re, h_im = scan(s5_step, (tuple(h0[:S5_TILES]), tuple(h0[S5_TILES:])), UNIT // 4)
    st_s5[...] = jnp.concatenate(list(h_re) + list(h_im), axis=1)

    h_all = jnp.concatenate([bu_scr[j] for j in range(2 * S5_TILES)], axis=1)
    y = _dot(h_all.astype(BF16), wc_ref[...]) + d_ref[...] * u_in
    neighbours(2)
    z = jax.nn.gelu(y)
    z = z * jax.nn.sigmoid(_dot(z.astype(BF16), wglu_ref[...]) + bglu_ref[...])
    neighbours(2)
    out_b = _rms(z, snorm_ref[...])

    tail = (CONV_W - 1) * SEQS
    xp_scr[CONV_PAD - tail:CONV_PAD, :] = st_cv[...]
    xp_scr[CONV_PAD:CONV_PAD + ROWS, :] = xr
    xc = cb_ref[...] + xr * cw_ref[CONV_W - 1:CONV_W, :]
    for j in range(CONV_W - 1):
        lo = CONV_PAD - tail + j * SEQS
        xc = xc + xp_scr[lo:lo + ROWS, :] * cw_ref[j:j + 1, :]
    st_cv[...] = xr[ROWS - tail:ROWS, :]
    gates = _dot(xc.astype(BF16), wax_ref[...]) + bax_ref[...]
    neighbours(2)
    r_gate = jax.nn.sigmoid(gates[:, 0:D_C])
    i_gate = jax.nn.sigmoid(gates[:, D_C:2 * D_C])
    log_a = lc_ref[...] * r_gate
    for j, tile in enumerate(_lane_tiles(jnp.exp(log_a))):
        la_scr[j] = tile
    neighbours(2)
    for j, tile in enumerate(_lane_tiles(jnp.sqrt(_neg_expm1(2.0 * log_a)) * (i_gate * xc))):
        lb_scr[j] = tile
    neighbours(1)

    def lru_step(t, h):
        rows = frame_rows(t)
        new = []
        for j in range(LRU_TILES):
            nh = la_scr[j, rows, :] * h[j] + lb_scr[j, rows, :]
            lb_scr[j, rows, :] = nh
            new.append(nh)
        return tuple(new)

    h_lru = scan(lru_step, tuple(_lane_tiles(st_lru[...])), UNIT // 2)
    st_lru[...] = jnp.concatenate(list(h_lru), axis=1)
    scanned = jnp.concatenate([out_b] + [lb_scr[j] for j in range(LRU_TILES)], axis=1)
    back = _dot(permt_ref[...], jnp.concatenate(_split3(scanned), axis=0))
    mix_scr[:, D_A:D_A + D_B] = back[:, 0:D_B]
    yc = back[:, D_B:D_B + D_C] * jax.nn.gelu(cols(COL_GR, D_C))
    mix_scr[:, D_A + D_B:D_MODEL] = _rms(yc, lnorm_ref[...])

    units = [slice(u * UNIT, (u + 1) * UNIT) for u in range(SEQS)]
    pairs = [slice(p * PAIR_W, (p + 1) * PAIR_W) for p in range(PAIRS)]
    lbv = lb_ref[...]
    rowi = lax.broadcasted_iota(jnp.int32, (ROWS, D_A), 0) % UNIT
    right0 = (rowi % (2 * SUB)) >= SUB
    right1 = rowi >= 2 * SUB
    head0 = lax.broadcasted_iota(jnp.int32, (ROWS, D_A), 1) % PAIR_W < HEAD
    tq = lax.broadcasted_iota(jnp.int32, (UNIT, PAIR_W), 0)
    sk = lax.broadcasted_iota(jnp.int32, (UNIT, PAIR_W), 1) % HEAD
    lag = tq - sk
    mask_d = lag * (lag - tq % SUB) <= 0
    mask_0 = (tq // (2 * SUB)) == (sk // (2 * SUB))
    ki = lax.broadcasted_iota(jnp.int32, (PAIR_W, PAIR_W), 0) // HEAD
    vi = lax.broadcasted_iota(jnp.int32, (PAIR_W, PAIR_W), 1) // HEAD
    same_head = ki == vi

    def by_head(x):
        return jnp.where(head0, x, 0.0).astype(BF16), jnp.where(head0, 0.0, x).astype(BF16)

    def stacked(halves, u, p):
        return jnp.concatenate([halves[0][units[u], pairs[p]], halves[1][units[u], pairs[p]]],
                               axis=0)

    neighbours(3)
    q = cols(COL_Q, D_A)
    fr = cols(COL_F, D_A)
    e = jnp.exp(-jnp.abs(fr))
    rcp = 1.0 / (1.0 + e)
    pos = fr >= 0.0
    sig = jnp.where(pos, rcp, e * rcp)
    sgm = jnp.where(pos, e * rcp, rcp)
    lf3 = _split3(jnp.log(lbv + (1.0 - lbv) * sig))
    k = (1.0 - lbv) * sgm
    g_cum = jnp.concatenate(
        [_dot(tri_ref[...], jnp.concatenate([part[rows] for part in lf3], axis=0))
         for rows in units], axis=0)
    neighbours(2)
    mids = [u * UNIT + SUB * i + SUB // 2 - 1 for u in range(SEQS) for i in range(UNIT // SUB)]
    dd = g_cum - _bcast_rows(g_cum, mids, SUB)
    q_d = (q * jnp.exp(dd)).astype(BF16)
    k_d = by_head(k * jnp.exp(-dd))
    neighbours(2)
    bounds = [u * UNIT + 2 * SUB * i + SUB - 1 for u in range(SEQS)
              for i in range(UNIT // (2 * SUB))]
    x0 = g_cum - _bcast_rows(g_cum, bounds, 2 * SUB)
    e0 = jnp.exp(jnp.where(right0, x0, -x0))
    q_0 = jnp.where(right0, q * e0, 0.0).astype(BF16)
    k_0 = by_head(jnp.where(right0, 0.0, k * e0))
    neighbours(2)
    x1 = g_cum - _bcast_rows(g_cum, [u * UNIT + 2 * SUB - 1 for u in range(SEQS)], UNIT)
    e1 = jnp.exp(jnp.where(right1, x1, -x1))
    q_1 = jnp.where(right1, q * e1, 0.0).astype(BF16)
    k_1 = by_head(jnp.where(right1, 0.0, k * e1))
    neighbours(2)
    g_end = _bcast_rows(g_cum, [u * UNIT + UNIT - 1 for u in range(SEQS)], UNIT)
    q_h = (q * jnp.exp(g_cum)).astype(BF16)
    k_h = (k * jnp.exp(g_end - g_cum)).astype(BF16)
    v = cols(COL_V, D_A)
    vb = v.astype(BF16)
    v_heads = by_head(v)
    neighbours(2)
    up = [(u, p) for u in range(SEQS) for p in range(PAIRS)]
    scores_d = [_dot_nt(q_d[units[u], pairs[p]], stacked(k_d, u, p)) for u, p in up]
    neighbours(1)
    scores_0 = [_dot_nt(q_0[units[u], pairs[p]], stacked(k_0, u, p)) for u, p in up]
    neighbours(1)
    scores_1 = [_dot_nt(q_1[units[u], pairs[p]], stacked(k_1, u, p)) for u, p in up]
    att = [(jnp.where(mask_d, s_d, 0.0) + jnp.where(mask_0, s_0, 0.0) + s_1).astype(BF16)
           for s_d, s_0, s_1 in zip(scores_d, scores_0, scores_1)]
    neighbours(2)
    states = [st_hg[u, p] for u, p in up]
    outs = [_dot(a, stacked(v_heads, u, p)) + _dot(q_h[units[u], pairs[p]], st.astype(BF16))
            for (u, p), a, st in zip(up, att, states)]
    updates = [_dot_tn(k_h[units[u], pairs[p]], vb[units[u], pairs[p]]) for u, p in up]
    for (u, p), o_up, st, upd in zip(up, outs, states, updates):
        o_scr[units[u], pairs[p]] = o_up
        dec = jnp.exp(g_cum[u * UNIT + UNIT - 1:u * UNIT + UNIT, pairs[p]])
        decay = jnp.broadcast_to(dec, (PAIR_W, PAIR_W)).T
        st_hg[u, p] = st * decay + jnp.where(same_head, upd, 0.0)
    neighbours(2)
    o = o_scr[...]
    o2 = o * o
    o2_hi = o2.astype(BF16)
    o2_lo = (o2 - o2_hi.astype(F32)).astype(BF16)
    msq = _dot(jnp.concatenate([o2_hi, o2_lo], axis=1), ones_ref[...]) * (1.0 / HEAD)
    on = o * lax.rsqrt(msq + EPS)
    mix_scr[:, 0:D_A] = on * hnorm_ref[...] * jax.nn.silu(cols(COL_G, D_A))
    for _ in stream:
        pass

    @pl.when(step == steps_per_seq - 1)
    def _():
        for u in range(SEQS):
            for p in range(PAIRS):
                st = st_hg[u, p]
                hgo_ref[u, 2 * p] = st[0:HEAD, 0:HEAD]
                hgo_ref[u, 2 * p + 1] = st[HEAD:PAIR_W, HEAD:PAIR_W]
            s5o_ref[u] = st_s5[u:u + 1, :]
            lruo_ref[u] = st_lru[u:u + 1, :]
            for j in range(CONV_W - 1):
                cvo_ref[u, j:j + 1, :] = st_cv[j * SEQS + u:j * SEQS + u + 1, :]


def _layer_constants():
    t = np.arange(UNIT)
    tri = (t[None, :] <= t[:, None]).astype(np.float32)
    tri3 = np.concatenate([tri, tri, tri], axis=1)
    h = np.arange(D_A) // HEAD
    ones = (h[:, None] == h[None, :]).astype(np.float32)
    ones2 = np.concatenate([ones, ones], axis=0)
    r = np.arange(ROWS)
    src = (r % SEQS) * UNIT + r // SEQS
    perm = (src[:, None] == r[None, :]).astype(np.float32)
    perm3 = np.concatenate([perm, perm, perm], axis=1)
    permt3 = np.concatenate([perm.T, perm.T, perm.T], axis=1)
    return [jnp.asarray(c, BF16) for c in (tri3, ones2, perm3, permt3)]


def _layer(x, states, weights, params, norm_final, layer, final):
    nseq, length, _ = x.shape
    steps_per_seq = length // UNIT
    steps = (nseq // SEQS) * steps_per_seq
    constants = _layer_constants()

    def block_map(i):
        return (i // steps_per_seq, i % steps_per_seq, 0)

    def group(i):
        return jnp.minimum(i, steps - 1) // steps_per_seq

    def state_spec(arr):
        nd = arr.ndim - 2
        return pl.BlockSpec((None, SEQS) + arr.shape[2:], lambda i: (layer, group(i)) + (0,) * nd)

    def out_spec(shape):
        nd = len(shape) - 1
        return pl.BlockSpec((SEQS,) + shape[1:], lambda i: (group(i),) + (0,) * nd)

    def const_spec(arr):
        nd = arr.ndim
        return pl.BlockSpec(arr.shape, lambda i: (0,) * nd, pipeline_mode=pl.Buffered(1))

    block = (SEQS, UNIT, D_MODEL)
    head = ["lb", "hnorm"]
    rest = ["wb", "a_re", "a_im", "wc", "d", "wglu", "bglu", "snorm",
            "cw", "cb", "wax", "bax", "lc", "lnorm"]
    ffn = ["w_out", "norm_ffn", "w_gate", "w_up", "w_down"]
    args = ([x, x, x] + list(states) + [weights["norm_mix"], weights["w_in"]]
            + [params[n] for n in head] + constants + [params[n] for n in rest]
            + [weights[n] for n in ffn] + [norm_final])
    in_specs = ([pl.BlockSpec(block, lambda i: (0, 0, 0), pipeline_mode=pl.Buffered(1)),
                 pl.BlockSpec(block, lambda i: block_map(jnp.minimum(i + 1, steps - 1))),
                 pl.BlockSpec(block, lambda i: block_map(jnp.maximum(i - 1, 0)))]
                + [state_spec(s) for s in states]
                + [_layer_spec(weights["norm_mix"], layer), _layer_spec(weights["w_in"], layer)]
                + [_layer_spec(params[n], layer) for n in head]
                + [const_spec(c) for c in constants]
                + [_layer_spec(params[n], layer) for n in rest]
                + [_layer_spec(weights[n], layer) for n in ffn]
                + [const_spec(norm_final)])
    state_shapes = [s.shape[1:] for s in states]
    out_shape = ((jax.ShapeDtypeStruct((nseq, length, D_MODEL), F32),)
                 + tuple(jax.ShapeDtypeStruct(s, F32) for s in state_shapes))
    scratch = [
        pltpu.VMEM((SEQS, PAIRS, PAIR_W, PAIR_W), F32),
        pltpu.VMEM((SEQS, 2 * S5_W), F32),
        pltpu.VMEM((SEQS, D_C), F32),
        pltpu.VMEM(((CONV_W - 1) * SEQS, D_C), F32),
        pltpu.VMEM((ROWS, D_IN), F32),
        pltpu.VMEM((ROWS, D_IN), F32),
        pltpu.VMEM((2 * S5_TILES, ROWS, LANES), F32),
        pltpu.VMEM((LRU_TILES, ROWS, LANES), F32),
        pltpu.VMEM((LRU_TILES, ROWS, LANES), F32),
        pltpu.VMEM((CONV_PAD + ROWS, D_C), F32),
        pltpu.VMEM((ROWS, D_A), F32),
        pltpu.VMEM((ROWS, D_MODEL), F32),
        pltpu.VMEM((ROWS, D_MODEL), F32),
        pltpu.VMEM((ROWS, D_MODEL), BF16),
        pltpu.VMEM((ROWS, D_FF), BF16),
    ]
    return pl.pallas_call(
        functools.partial(_layer_kernel, steps_per_seq, final),
        grid=(steps + 1,),
        in_specs=in_specs,
        out_specs=([pl.BlockSpec(block, lambda i: block_map(jnp.maximum(i - 1, 0)))]
                   + [out_spec(s) for s in state_shapes]),
        out_shape=out_shape,
        scratch_shapes=scratch,
        compiler_params=pltpu.CompilerParams(
            dimension_semantics=("arbitrary",), vmem_limit_bytes=VMEM_LIMIT),
        name="layer",
    )(*args)


def _block_diag(w):
    l, h, i, j = w.shape
    return jnp.einsum("lhij,hk->lhikj", w, jnp.eye(h, dtype=w.dtype)).reshape(l, h * i, h * j)


def _mixer_params(consts, s5_c_re, s5_c_im, s5_d, s5_w_glu, s5_b_glu, s5_norm, hgrn_norm,
                  lru_conv_w, lru_conv_b, lru_wa, lru_ba, lru_wx, lru_bx, lru_norm):
    lbs, a_re, a_im, bbar_re, bbar_im, lru_c = consts
    depth = lbs.shape[0]
    eye = jnp.eye(S5_GROUPS, dtype=F32)
    row = lambda x, w: x.reshape(depth, 1, w)
    embed_b = lambda b: jnp.einsum("lgnp,gh->lgnhp", b, eye).reshape(depth, D_B, S5_W)
    embed_c = lambda c: jnp.einsum("lgnp,gh->lgphn", c, eye).reshape(depth, S5_W, D_B)
    return {
        "lb": row(lbs, D_A), "hnorm": row(hgrn_norm, D_A),
        "wb": jnp.concatenate([embed_b(bbar_re), embed_b(bbar_im)], axis=2).astype(BF16),
        "a_re": row(a_re, S5_W), "a_im": row(a_im, S5_W),
        "wc": jnp.concatenate([embed_c(s5_c_re), embed_c(-s5_c_im)], axis=1).astype(BF16),
        "d": row(s5_d, D_B), "wglu": s5_w_glu.astype(BF16), "bglu": row(s5_b_glu, D_B),
        "snorm": row(s5_norm, D_B), "cw": lru_conv_w, "cb": row(lru_conv_b, D_C),
        "wax": jnp.concatenate([_block_diag(lru_wa), _block_diag(lru_wx)], axis=2).astype(BF16),
        "bax": jnp.concatenate([lru_ba, lru_bx], axis=1).reshape(depth, 1, 2 * D_C),
        "lc": row(lru_c, D_C), "lnorm": row(lru_norm, D_C),
    }


def kernel(x_prompt, x_sample, state_hgrn, state_s5_re, state_s5_im, state_rglru, cache_conv, norm_mix, w_in, hgrn_lb, hgrn_norm, s5_lam_re, s5_lam_im, s5_log_dt, s5_b_re, s5_b_im, s5_c_re, s5_c_im, s5_d, s5_w_glu, s5_b_glu, s5_norm, lru_conv_w, lru_conv_b, lru_wa, lru_ba, lru_wx, lru_bx, lru_lam, lru_norm, w_out, norm_ffn, w_ffn_gate, w_ffn_up, w_ffn_down, norm_final):
    depth = w_in.shape[0]
    consts = _prep(hgrn_lb, s5_lam_re, s5_lam_im, s5_log_dt, s5_b_re, s5_b_im, lru_lam)
    params = _mixer_params(consts, s5_c_re, s5_c_im, s5_d, s5_w_glu, s5_b_glu, s5_norm, hgrn_norm,
                           lru_conv_w, lru_conv_b, lru_wa, lru_ba, lru_wx, lru_bx, lru_norm)
    weights = {
        "norm_mix": norm_mix.reshape(depth, 1, D_MODEL), "w_in": w_in.astype(BF16),
        "w_out": w_out.astype(BF16), "norm_ffn": norm_ffn.reshape(depth, 1, D_MODEL),
        "w_gate": w_ffn_gate.astype(BF16), "w_up": w_ffn_up.astype(BF16),
        "w_down": w_ffn_down.astype(BF16),
    }
    norm_final2 = norm_final.reshape(1, D_MODEL)

    def trunk(x, s_hgrn, s_re, s_im, s_lru, s_conv):
        nseq = x.shape[0]
        states = (s_hgrn,
                  jnp.concatenate([s_re.reshape(depth, nseq, 1, S5_W),
                                   s_im.reshape(depth, nseq, 1, S5_W)], axis=-1),
                  s_lru.reshape(depth, nseq, 1, D_C), s_conv)
        outs = []
        for l in range(depth):
            x, hg, s5, lru, cv = _layer(x, states, weights, params, norm_final2, l, l == depth - 1)
            outs.append((hg, s5, lru, cv))
        hg, s5, lru, cv = (jnp.stack([o[i] for o in outs]) for i in range(4))
        return (x, hg,
                s5[:, :, 0, 0:S5_W].reshape(depth, nseq, S5_GROUPS, S5_STATE),
                s5[:, :, 0, S5_W:].reshape(depth, nseq, S5_GROUPS, S5_STATE),
                lru[:, :, 0, :], cv)

    bp = x_prompt.shape[0]
    zeros = lambda *shape: jnp.zeros((depth, bp) + shape, F32)
    res_p = trunk(x_prompt, zeros(HEADS, HEAD, HEAD), zeros(S5_GROUPS, S5_STATE),
                  zeros(S5_GROUPS, S5_STATE), zeros(D_C), zeros(CONV_W - 1, D_C))
    res_s = trunk(x_sample, state_hgrn, state_s5_re, state_s5_im, state_rglru, cache_conv)
    return (res_p[0], res_s[0]) + res_p[1:] + res_s[1:]
```

```python
import functools

import numpy as np
import jax
import jax.numpy as jnp
from jax import lax
from jax.experimental import pallas as pl
from jax.experimental.pallas import tpu as pltpu

F32 = jnp.float32
BF16 = jnp.bfloat16

D_MODEL = 1024
D_A = 384
D_B = 256
D_C = 384
HEAD = 64
HEADS = D_A // HEAD
PAIRS = HEADS // 2
PAIR_W = 2 * HEAD
S5_GROUPS = 16
S5_GROUP = 16
S5_STATE = 64
S5_W = S5_GROUPS * S5_STATE
LRU_HEADS = 6
LRU_BLOCK = 64
CONV_W = 4
LRU_C = 8.0
D_FF = 2816
D_IN = 4 * D_A + D_B + 2 * D_C
EPS = 1e-6

LANES = 128
UNIT = 64
SEQS = 4
ROWS = UNIT * SEQS
SUB = 16
CONV_PAD = 16
MM_CHUNK = 256
EARLY_CHUNKS = 2
S5_TILES = S5_W // LANES
LRU_TILES = D_C // LANES

COL_Q, COL_F, COL_V, COL_G = 0, D_A, 2 * D_A, 3 * D_A
COL_U = 4 * D_A
COL_XR = COL_U + D_B
COL_GR = COL_XR + D_C

VMEM_LIMIT = 60 * 1024 * 1024


def _dot(a, b):
    return jnp.dot(a, b, preferred_element_type=F32)


def _dot_nt(a, b):
    return lax.dot_general(a, b, (((1,), (1,)), ((), ())), preferred_element_type=F32)


def _dot_tn(a, b):
    return lax.dot_general(a, b, (((0,), (0,)), ((), ())), preferred_element_type=F32)


def _neg_expm1(x):
    u = jnp.exp(x)
    near = jnp.where(u == 1.0, -x, (1.0 - u) * x / jnp.log(u))
    return jnp.where(x < -0.5, 1.0 - u, near)


def _rms(x, w):
    return x * lax.rsqrt(jnp.mean(x * x, axis=-1, keepdims=True) + EPS) * w


def _layer_spec(arr, layer):
    nd = arr.ndim
    return pl.BlockSpec((None,) + arr.shape[1:], lambda *_: (layer,) + (0,) * (nd - 1),
                        pipeline_mode=pl.Buffered(1))


def _prep_kernel(lb_ref, lam_re_ref, lam_im_ref, log_dt_ref, bt_re_ref, bt_im_ref, lru_lam_ref,
                 lbs_ref, a_re_ref, a_im_ref, bbar_re_ref, bbar_im_ref, lru_c_ref):
    depth = lb_ref.shape[0]
    rows = [lb_ref[l:l + 1, :] for l in range(depth)]
    m = rows[0]
    for r in rows[1:]:
        m = jnp.maximum(m, r)
    es = [jnp.exp(r - m) for r in rows]
    tot = es[0]
    for e in es[1:]:
        tot = tot + e
    cs = None
    for l in range(depth):
        sm = es[l] / tot
        cs = sm if cs is None else cs + sm
        if l == 0:
            cs0 = cs
        lbs_ref[l:l + 1, :] = cs - cs0

    dt = jnp.exp(log_dt_ref[...])
    lr = jnp.minimum(lam_re_ref[...], -1e-4)
    li = lam_im_ref[...]
    mag = jnp.exp(lr * dt)
    a_re = mag * jnp.cos(li * dt)
    a_im = mag * jnp.sin(li * dt)
    den = lr * lr + li * li
    nr = a_re - 1.0
    gam_re = (nr * lr + a_im * li) / den
    gam_im = (a_im * lr - nr * li) / den
    a_re_ref[...] = a_re
    a_im_ref[...] = a_im
    b_re = bt_re_ref[...]
    b_im = bt_im_ref[...]
    bbar_re_ref[...] = gam_re * b_re - gam_im * b_im
    bbar_im_ref[...] = gam_re * b_im + gam_im * b_re

    z = -lru_lam_ref[...]
    softplus = jnp.maximum(z, 0.0) + jnp.log1p(jnp.exp(-jnp.abs(z)))
    lru_c_ref[...] = -LRU_C * softplus


def _prep(hgrn_lb, lam_re, lam_im, log_dt, b_re, b_im, lru_lam):
    depth = hgrn_lb.shape[0]
    g, p, n = S5_GROUPS, S5_STATE, S5_GROUP
    out_shape = (
        jax.ShapeDtypeStruct((depth, D_A), F32),
        jax.ShapeDtypeStruct((depth, g, 1, p), F32),
        jax.ShapeDtypeStruct((depth, g, 1, p), F32),
        jax.ShapeDtypeStruct((depth, g, n, p), F32),
        jax.ShapeDtypeStruct((depth, g, n, p), F32),
        jax.ShapeDtypeStruct((depth, D_C), F32),
    )
    return pl.pallas_call(_prep_kernel, out_shape=out_shape, name="prep")(
        hgrn_lb, lam_re.reshape(depth, g, 1, p), lam_im.reshape(depth, g, 1, p),
        log_dt.reshape(depth, g, 1, 1), jnp.swapaxes(b_re, -1, -2), jnp.swapaxes(b_im, -1, -2),
        lru_lam)


def _split3(x):
    hi = x.astype(BF16)
    r1 = x - hi.astype(F32)
    mid = r1.astype(BF16)
    lo = (r1 - mid.astype(F32)).astype(BF16)
    return hi, mid, lo


def _split2(x):
    hi = x.astype(BF16)
    return hi, (x - hi.astype(F32)).astype(BF16)


def _bcast_rows(x, idxs, reps):
    width = x.shape[1]
    return jnp.concatenate(
        [jnp.broadcast_to(x[i:i + 1, :], (reps, width)) for i in idxs], axis=0)


def _lane_tiles(x):
    return [x[:, j * LANES:(j + 1) * LANES] for j in range(x.shape[1] // LANES)]


def _layer_kernel(steps_per_seq, final,
                  x0_ref, xnext_ref, xprev_ref, hg0_ref, s50_ref, lru0_ref, cv0_ref,
                  nmix_ref, win_ref,
                  lb_ref, hnorm_ref, tri_ref, ones_ref, perm_ref, permt_ref,
                  wb_ref, are_ref, aim_ref, wc_ref, d_ref, wglu_ref, bglu_ref, snorm_ref,
                  cw_ref, cb_ref, wax_ref, bax_ref, lc_ref, lnorm_ref,
                  wo_ref, nffn_ref, wg_ref, wu_ref, wd_ref, nfin_ref,
                  out_ref, hgo_ref, s5o_ref, lruo_ref, cvo_ref,
                  st_hg, st_s5, st_lru, st_cv, proj_scr, pnext_scr, bu_scr, la_scr, lb_scr, xp_scr,
                  o_scr, mix_scr, x1_scr, xn_scr, hid_scr):
    step = pl.program_id(0) % steps_per_seq

    @pl.when(pl.program_id(0) == 0)
    def _():
        xn = _rms(x0_ref[...].reshape(ROWS, D_MODEL), nmix_ref[...])
        pnext_scr[...] = _dot(xn.astype(BF16), win_ref[...])
        mix_scr[...] = jnp.zeros((ROWS, D_MODEL), F32)

    zero_head = jnp.zeros((HEAD, HEAD), F32)

    @pl.when(step == 0)
    def _():
        for u in range(SEQS):
            for p in range(PAIRS):
                st_hg[u, p] = jnp.concatenate([
                    jnp.concatenate([hg0_ref[u, 2 * p], zero_head], axis=1),
                    jnp.concatenate([zero_head, hg0_ref[u, 2 * p + 1]], axis=1)], axis=0)
            for j in range(CONV_W - 1):
                st_cv[j * SEQS + u:j * SEQS + u + 1, :] = cv0_ref[u, j:j + 1, :]
        st_s5[...] = jnp.concatenate([s50_ref[u] for u in range(SEQS)], axis=0)
        st_lru[...] = jnp.concatenate([lru0_ref[u] for u in range(SEQS)], axis=0)

    def neighbour_stream():
        chunk = lambda c: slice(c * MM_CHUNK, (c + 1) * MM_CHUNK)
        mix_prev = mix_scr[...].astype(BF16)

        def out_projection(c):
            x1_scr[:, chunk(c)] = (xprev_ref[:, :, chunk(c)].reshape(ROWS, MM_CHUNK)
                                   + _dot(mix_prev, wo_ref[:, chunk(c)]))

        for c in range(EARLY_CHUNKS):
            out_projection(c)
            yield
        x_next = _rms(xnext_ref[...].reshape(ROWS, D_MODEL), nmix_ref[...]).astype(BF16)
        for c in range(D_IN // MM_CHUNK):
            pnext_scr[:, chunk(c)] = _dot(x_next, win_ref[:, chunk(c)])
            yield
        for c in range(EARLY_CHUNKS, D_MODEL // MM_CHUNK):
            out_projection(c)
            yield
        xn_scr[...] = _rms(x1_scr[...], nffn_ref[...]).astype(BF16)
        yield
        for c in range(D_FF // MM_CHUNK):
            xn = xn_scr[...]
            gate = _dot(xn, wg_ref[:, chunk(c)])
            yield
            up = _dot(xn, wu_ref[:, chunk(c)])
            hid_scr[:, chunk(c)] = (jax.nn.silu(gate) * up).astype(BF16)
            yield
        for c in range(D_MODEL // MM_CHUNK):
            x2 = x1_scr[:, chunk(c)] + _dot(hid_scr[...], wd_ref[:, chunk(c)])
            if final:
                x1_scr[:, chunk(c)] = x2
            else:
                out_ref[:, :, chunk(c)] = x2.reshape(SEQS, UNIT, MM_CHUNK)
            yield
        if final:
            out_ref[...] = _rms(x1_scr[...], nfin_ref[...]).reshape(SEQS, UNIT, D_MODEL)

    stream = neighbour_stream()

    def neighbours(n):
        for _ in range(n):
            next(stream, None)

    def cols(lo, width):
        return proj_scr[:, lo:lo + width]

    neighbours(EARLY_CHUNKS)
    proj_scr[:, 0:COL_U] = pnext_scr[:, 0:COL_U]
    proj_scr[:, COL_GR:D_IN] = pnext_scr[:, COL_GR:D_IN]
    tm_parts = jnp.concatenate(_split2(pnext_scr[:, COL_U:COL_GR]), axis=0)
    neighbours(1)
    tm = _dot(perm_ref[...], tm_parts)
    neighbours(1)
    u_in = tm[:, 0:D_B]
    xr = tm[:, D_B:D_B + D_C]
    frame_rows = lambda t: slice(t * SEQS, (t + 1) * SEQS)

    def scan(step_fn, carry, gap):
        for t in range(UNIT):
            carry = step_fn(t, carry)
            if t % gap == gap - 1:
                neighbours(1)
        return carry

    bu = _dot(u_in.astype(BF16), wb_ref[...])
    neighbours(1)
    for j, tile in enumerate(_lane_tiles(bu)):
        bu_scr[j] = tile
    neighbours(2)
    a_re = [jnp.broadcast_to(a, (SEQS, LANES)) for a in _lane_tiles(are_ref[...])]
    a_im = [jnp.broadcast_to(a, (SEQS, LANES)) for a in _lane_tiles(aim_ref[...])]

    def s5_step(t, h):
        rows = frame_rows(t)
        new_re, new_im = [], []
        for j in range(S5_TILES):
            hr, hi = h[0][j], h[1][j]
            nr = a_re[j] * hr - a_im[j] * hi + bu_scr[j, rows, :]
            ni = a_re[j] * hi + a_im[j] * hr + bu_scr[S5_TILES + j, rows, :]
            bu_scr[j, rows, :] = nr
            bu_scr[S5_TILES + j, rows, :] = ni
            new_re.append(nr)
            new_im.append(ni)
        return tuple(new_re), tuple(new_im)

    h0 = _lane_tiles(st_s5[...])
    h_re, h_im = scan(s5_step, (tuple(h0[:S5_TILES]), tuple(h0[S5_TILES:])), UNIT // 4)
    st_s5[...] = jnp.concatenate(list(h_re) + list(h_im), axis=1)

    h_all = jnp.concatenate([bu_scr[j] for j in range(2 * S5_TILES)], axis=1)
    y = _dot(h_all.astype(BF16), wc_ref[...]) + d_ref[...] * u_in
    neighbours(2)
    z = jax.nn.gelu(y)
    z = z * jax.nn.sigmoid(_dot(z.astype(BF16), wglu_ref[...]) + bglu_ref[...])
    neighbours(2)
    out_b = _rms(z, snorm_ref[...])

    tail = (CONV_W - 1) * SEQS
    xp_scr[CONV_PAD - tail:CONV_PAD, :] = st_cv[...]
    xp_scr[CONV_PAD:CONV_PAD + ROWS, :] = xr
    xc = cb_ref[...] + xr * cw_ref[CONV_W - 1:CONV_W, :]
    for j in range(CONV_W - 1):
        lo = CONV_PAD - tail + j * SEQS
        xc = xc + xp_scr[lo:lo + ROWS, :] * cw_ref[j:j + 1, :]
    st_cv[...] = xr[ROWS - tail:ROWS, :]
    gates = _dot(xc.astype(BF16), wax_ref[...]) + bax_ref[...]
    neighbours(2)
    r_gate = jax.nn.sigmoid(gates[:, 0:D_C])
    i_gate = jax.nn.sigmoid(gates[:, D_C:2 * D_C])
    log_a = lc_ref[...] * r_gate
    for j, tile in enumerate(_lane_tiles(jnp.exp(log_a))):
        la_scr[j] = tile
    neighbours(2)
    for j, tile in enumerate(_lane_tiles(jnp.sqrt(_neg_expm1(2.0 * log_a)) * (i_gate * xc))):
        lb_scr[j] = tile
    neighbours(1)

    def lru_step(t, h):
        rows = frame_rows(t)
        new = []
        for j in range(LRU_TILES):
            nh = la_scr[j, rows, :] * h[j] + lb_scr[j, rows, :]
            lb_scr[j, rows, :] = nh
            new.append(nh)
        return tuple(new)

    h_lru = scan(lru_step, tuple(_lane_tiles(st_lru[...])), UNIT // 2)
    st_lru[...] = jnp.concatenate(list(h_lru), axis=1)
    scanned = jnp.concatenate([out_b] + [lb_scr[j] for j in range(LRU_TILES)], axis=1)
    back = _dot(permt_ref[...], jnp.concatenate(_split2(scanned), axis=0))
    mix_scr[:, D_A:D_A + D_B] = back[:, 0:D_B]
    yc = back[:, D_B:D_B + D_C] * jax.nn.gelu(cols(COL_GR, D_C))
    mix_scr[:, D_A + D_B:D_MODEL] = _rms(yc, lnorm_ref[...])

    units = [slice(u * UNIT, (u + 1) * UNIT) for u in range(SEQS)]
    pairs = [slice(p * PAIR_W, (p + 1) * PAIR_W) for p in range(PAIRS)]
    lbv = lb_ref[...]
    rowi = lax.broadcasted_iota(jnp.int32, (ROWS, D_A), 0) % UNIT
    right0 = (rowi % (2 * SUB)) >= SUB
    right1 = rowi >= 2 * SUB
    head0 = lax.broadcasted_iota(jnp.int32, (ROWS, D_A), 1) % PAIR_W < HEAD
    tq = lax.broadcasted_iota(jnp.int32, (UNIT, PAIR_W), 0)
    sk = lax.broadcasted_iota(jnp.int32, (UNIT, PAIR_W), 1) % HEAD
    lag = tq - sk
    mask_d = lag * (lag - tq % SUB) <= 0
    mask_0 = (tq // (2 * SUB)) == (sk // (2 * SUB))
    ki = lax.broadcasted_iota(jnp.int32, (PAIR_W, PAIR_W), 0) // HEAD
    vi = lax.broadcasted_iota(jnp.int32, (PAIR_W, PAIR_W), 1) // HEAD
    same_head = ki == vi

    def by_head(x):
        return jnp.where(head0, x, 0.0).astype(BF16), jnp.where(head0, 0.0, x).astype(BF16)

    def stacked(halves, u, p):
        return jnp.concatenate([halves[0][units[u], pairs[p]], halves[1][units[u], pairs[p]]],
                               axis=0)

    neighbours(3)
    q = cols(COL_Q, D_A)
    fr = cols(COL_F, D_A)
    e = jnp.exp(-jnp.abs(fr))
    rcp = 1.0 / (1.0 + e)
    pos = fr >= 0.0
    sig = jnp.where(pos, rcp, e * rcp)
    sgm = jnp.where(pos, e * rcp, rcp)
    lf3 = _split3(jnp.log(lbv + (1.0 - lbv) * sig))
    k = (1.0 - lbv) * sgm
    g_cum = jnp.concatenate(
        [_dot(tri_ref[...], jnp.concatenate([part[rows] for part in lf3], axis=0))
         for rows in units], axis=0)
    neighbours(2)
    mids = [u * UNIT + SUB * i + SUB // 2 - 1 for u in range(SEQS) for i in range(UNIT // SUB)]
    dd = g_cum - _bcast_rows(g_cum, mids, SUB)
    q_d = (q * jnp.exp(dd)).astype(BF16)
    k_d = by_head(k * jnp.exp(-dd))
    neighbours(2)
    bounds = [u * UNIT + 2 * SUB * i + SUB - 1 for u in range(SEQS)
              for i in range(UNIT // (2 * SUB))]
    x0 = g_cum - _bcast_rows(g_cum, bounds, 2 * SUB)
    e0 = jnp.exp(jnp.where(right0, x0, -x0))
    q_0 = jnp.where(right0, q * e0, 0.0).astype(BF16)
    k_0 = by_head(jnp.where(right0, 0.0, k * e0))
    neighbours(2)
    x1 = g_cum - _bcast_rows(g_cum, [u * UNIT + 2 * SUB - 1 for u in range(SEQS)], UNIT)
    e1 = jnp.exp(jnp.where(right1, x1, -x1))
    q_1 = jnp.where(right1, q * e1, 0.0).astype(BF16)
    k_1 = by_head(jnp.where(right1, 0.0, k * e1))
    neighbours(2)
    g_end = _bcast_rows(g_cum, [u * UNIT + UNIT - 1 for u in range(SEQS)], UNIT)
    q_h = (q * jnp.exp(g_cum)).astype(BF16)
    k_h = (k * jnp.exp(g_end - g_cum)).astype(BF16)
    v = cols(COL_V, D_A)
    vb = v.astype(BF16)
    v_heads = by_head(v)
    neighbours(2)
    up = [(u, p) for u in range(SEQS) for p in range(PAIRS)]
    scores_d = [_dot_nt(q_d[units[u], pairs[p]], stacked(k_d, u, p)) for u, p in up]
    neighbours(1)
    scores_0 = [_dot_nt(q_0[units[u], pairs[p]], stacked(k_0, u, p)) for u, p in up]
    neighbours(1)
    scores_1 = [_dot_nt(q_1[units[u], pairs[p]], stacked(k_1, u, p)) for u, p in up]
    att = [(jnp.where(mask_d, s_d, 0.0) + jnp.where(mask_0, s_0, 0.0) + s_1).astype(BF16)
           for s_d, s_0, s_1 in zip(scores_d, scores_0, scores_1)]
    neighbours(2)
    states = [st_hg[u, p] for u, p in up]
    outs = [_dot(a, stacked(v_heads, u, p)) + _dot(q_h[units[u], pairs[p]], st.astype(BF16))
            for (u, p), a, st in zip(up, att, states)]
    updates = [_dot_tn(k_h[units[u], pairs[p]], vb[units[u], pairs[p]]) for u, p in up]
    for (u, p), o_up, st, upd in zip(up, outs, states, updates):
        o_scr[units[u], pairs[p]] = o_up
        dec = jnp.exp(g_cum[u * UNIT + UNIT - 1:u * UNIT + UNIT, pairs[p]])
        decay = jnp.broadcast_to(dec, (PAIR_W, PAIR_W)).T
        st_hg[u, p] = st * decay + jnp.where(same_head, upd, 0.0)
    o = o_scr[...]
    o2 = o * o
    o2_hi = o2.astype(BF16)
    o2_lo = (o2 - o2_hi.astype(F32)).astype(BF16)
    msq = _dot(jnp.concatenate([o2_hi, o2_lo], axis=1), ones_ref[...]) * (1.0 / HEAD)
    on = o * lax.rsqrt(msq + EPS)
    mix_scr[:, 0:D_A] = on * hnorm_ref[...] * jax.nn.silu(cols(COL_G, D_A))
    for _ in stream:
        pass

    @pl.when(step == steps_per_seq - 1)
    def _():
        for u in range(SEQS):
            for p in range(PAIRS):
                st = st_hg[u, p]
                hgo_ref[u, 2 * p] = st[0:HEAD, 0:HEAD]
                hgo_ref[u, 2 * p + 1] = st[HEAD:PAIR_W, HEAD:PAIR_W]
            s5o_ref[u] = st_s5[u:u + 1, :]
            lruo_ref[u] = st_lru[u:u + 1, :]
            for j in range(CONV_W - 1):
                cvo_ref[u, j:j + 1, :] = st_cv[j * SEQS + u:j * SEQS + u + 1, :]


def _layer_constants():
    t = np.arange(UNIT)
    tri = (t[None, :] <= t[:, None]).astype(np.float32)
    tri3 = np.concatenate([tri, tri, tri], axis=1)
    h = np.arange(D_A) // HEAD
    ones = (h[:, None] == h[None, :]).astype(np.float32)
    ones2 = np.concatenate([ones, ones], axis=0)
    r = np.arange(ROWS)
    src = (r % SEQS) * UNIT + r // SEQS
    perm = (src[:, None] == r[None, :]).astype(np.float32)
    perm2 = np.concatenate([perm, perm], axis=1)
    permt2 = np.concatenate([perm.T, perm.T], axis=1)
    return [jnp.asarray(c, BF16) for c in (tri3, ones2, perm2, permt2)]


def _layer(x, states, weights, params, norm_final, layer, final):
    nseq, length, _ = x.shape
    steps_per_seq = length // UNIT
    steps = (nseq // SEQS) * steps_per_seq
    constants = _layer_constants()

    def block_map(i):
        return (i // steps_per_seq, i % steps_per_seq, 0)

    def group(i):
        return jnp.minimum(i, steps - 1) // steps_per_seq

    def state_spec(arr):
        nd = arr.ndim - 2
        return pl.BlockSpec((None, SEQS) + arr.shape[2:], lambda i: (layer, group(i)) + (0,) * nd)

    def out_spec(shape):
        nd = len(shape) - 1
        return pl.BlockSpec((SEQS,) + shape[1:], lambda i: (group(i),) + (0,) * nd)

    def const_spec(arr):
        nd = arr.ndim
        return pl.BlockSpec(arr.shape, lambda i: (0,) * nd, pipeline_mode=pl.Buffered(1))

    block = (SEQS, UNIT, D_MODEL)
    head = ["lb", "hnorm"]
    rest = ["wb", "a_re", "a_im", "wc", "d", "wglu", "bglu", "snorm",
            "cw", "cb", "wax", "bax", "lc", "lnorm"]
    ffn = ["w_out", "norm_ffn", "w_gate", "w_up", "w_down"]
    args = ([x, x, x] + list(states) + [weights["norm_mix"], weights["w_in"]]
            + [params[n] for n in head] + constants + [params[n] for n in rest]
            + [weights[n] for n in ffn] + [norm_final])
    in_specs = ([pl.BlockSpec(block, lambda i: (0, 0, 0), pipeline_mode=pl.Buffered(1)),
                 pl.BlockSpec(block, lambda i: block_map(jnp.minimum(i + 1, steps - 1))),
                 pl.BlockSpec(block, lambda i: block_map(jnp.maximum(i - 1, 0)))]
                + [state_spec(s) for s in states]
                + [_layer_spec(weights["norm_mix"], layer), _layer_spec(weights["w_in"], layer)]
                + [_layer_spec(params[n], layer) for n in head]
                + [const_spec(c) for c in constants]
                + [_layer_spec(params[n], layer) for n in rest]
                + [_layer_spec(weights[n], layer) for n in ffn]
                + [const_spec(norm_final)])
    state_shapes = [s.shape[1:] for s in states]
    out_shape = ((jax.ShapeDtypeStruct((nseq, length, D_MODEL), F32),)
                 + tuple(jax.ShapeDtypeStruct(s, F32) for s in state_shapes))
    scratch = [
        pltpu.VMEM((SEQS, PAIRS, PAIR_W, PAIR_W), F32),
        pltpu.VMEM((SEQS, 2 * S5_W), F32),
        pltpu.VMEM((SEQS, D_C), F32),
        pltpu.VMEM(((CONV_W - 1) * SEQS, D_C), F32),
        pltpu.VMEM((ROWS, D_IN), F32),
        pltpu.VMEM((ROWS, D_IN), F32),
        pltpu.VMEM((2 * S5_TILES, ROWS, LANES), F32),
        pltpu.VMEM((LRU_TILES, ROWS, LANES), F32),
        pltpu.VMEM((LRU_TILES, ROWS, LANES), F32),
        pltpu.VMEM((CONV_PAD + ROWS, D_C), F32),
        pltpu.VMEM((ROWS, D_A), F32),
        pltpu.VMEM((ROWS, D_MODEL), F32),
        pltpu.VMEM((ROWS, D_MODEL), F32),
        pltpu.VMEM((ROWS, D_MODEL), BF16),
        pltpu.VMEM((ROWS, D_FF), BF16),
    ]
    return pl.pallas_call(
        functools.partial(_layer_kernel, steps_per_seq, final),
        grid=(steps + 1,),
        in_specs=in_specs,
        out_specs=([pl.BlockSpec(block, lambda i: block_map(jnp.maximum(i - 1, 0)))]
                   + [out_spec(s) for s in state_shapes]),
        out_shape=out_shape,
        scratch_shapes=scratch,
        compiler_params=pltpu.CompilerParams(
            dimension_semantics=("arbitrary",), vmem_limit_bytes=VMEM_LIMIT),
        name="layer",
    )(*args)


def _block_diag(w):
    l, h, i, j = w.shape
    return jnp.einsum("lhij,hk->lhikj", w, jnp.eye(h, dtype=w.dtype)).reshape(l, h * i, h * j)


def _mixer_params(consts, s5_c_re, s5_c_im, s5_d, s5_w_glu, s5_b_glu, s5_norm, hgrn_norm,
                  lru_conv_w, lru_conv_b, lru_wa, lru_ba, lru_wx, lru_bx, lru_norm):
    lbs, a_re, a_im, bbar_re, bbar_im, lru_c = consts
    depth = lbs.shape[0]
    eye = jnp.eye(S5_GROUPS, dtype=F32)
    row = lambda x, w: x.reshape(depth, 1, w)
    embed_b = lambda b: jnp.einsum("lgnp,gh->lgnhp", b, eye).reshape(depth, D_B, S5_W)
    embed_c = lambda c: jnp.einsum("lgnp,gh->lgphn", c, eye).reshape(depth, S5_W, D_B)
    return {
        "lb": row(lbs, D_A), "hnorm": row(hgrn_norm, D_A),
        "wb": jnp.concatenate([embed_b(bbar_re), embed_b(bbar_im)], axis=2).astype(BF16),
        "a_re": row(a_re, S5_W), "a_im": row(a_im, S5_W),
        "wc": jnp.concatenate([embed_c(s5_c_re), embed_c(-s5_c_im)], axis=1).astype(BF16),
        "d": row(s5_d, D_B), "wglu": s5_w_glu.astype(BF16), "bglu": row(s5_b_glu, D_B),
        "snorm": row(s5_norm, D_B), "cw": lru_conv_w, "cb": row(lru_conv_b, D_C),
        "wax": jnp.concatenate([_block_diag(lru_wa), _block_diag(lru_wx)], axis=2).astype(BF16),
        "bax": jnp.concatenate([lru_ba, lru_bx], axis=1).reshape(depth, 1, 2 * D_C),
        "lc": row(lru_c, D_C), "lnorm": row(lru_norm, D_C),
    }


def kernel(x_prompt, x_sample, state_hgrn, state_s5_re, state_s5_im, state_rglru, cache_conv, norm_mix, w_in, hgrn_lb, hgrn_norm, s5_lam_re, s5_lam_im, s5_log_dt, s5_b_re, s5_b_im, s5_c_re, s5_c_im, s5_d, s5_w_glu, s5_b_glu, s5_norm, lru_conv_w, lru_conv_b, lru_wa, lru_ba, lru_wx, lru_bx, lru_lam, lru_norm, w_out, norm_ffn, w_ffn_gate, w_ffn_up, w_ffn_down, norm_final):
    depth = w_in.shape[0]
    consts = _prep(hgrn_lb, s5_lam_re, s5_lam_im, s5_log_dt, s5_b_re, s5_b_im, lru_lam)
    params = _mixer_params(consts, s5_c_re, s5_c_im, s5_d, s5_w_glu, s5_b_glu, s5_norm, hgrn_norm,
                           lru_conv_w, lru_conv_b, lru_wa, lru_ba, lru_wx, lru_bx, lru_norm)
    weights = {
        "norm_mix": norm_mix.reshape(depth, 1, D_MODEL), "w_in": w_in.astype(BF16),
        "w_out": w_out.astype(BF16), "norm_ffn": norm_ffn.reshape(depth, 1, D_MODEL),
        "w_gate": w_ffn_gate.astype(BF16), "w_up": w_ffn_up.astype(BF16),
        "w_down": w_ffn_down.astype(BF16),
    }
    norm_final2 = norm_final.reshape(1, D_MODEL)

    def trunk(x, s_hgrn, s_re, s_im, s_lru, s_conv):
        nseq = x.shape[0]
        states = (s_hgrn,
                  jnp.concatenate([s_re.reshape(depth, nseq, 1, S5_W),
                                   s_im.reshape(depth, nseq, 1, S5_W)], axis=-1),
                  s_lru.reshape(depth, nseq, 1, D_C), s_conv)
        outs = []
        for l in range(depth):
            x, hg, s5, lru, cv = _layer(x, states, weights, params, norm_final2, l, l == depth - 1)
            outs.append((hg, s5, lru, cv))
        hg, s5, lru, cv = (jnp.stack([o[i] for o in outs]) for i in range(4))
        return (x, hg,
                s5[:, :, 0, 0:S5_W].reshape(depth, nseq, S5_GROUPS, S5_STATE),
                s5[:, :, 0, S5_W:].reshape(depth, nseq, S5_GROUPS, S5_STATE),
                lru[:, :, 0, :], cv)

    bp = x_prompt.shape[0]
    zeros = lambda *shape: jnp.zeros((depth, bp) + shape, F32)
    res_p = trunk(x_prompt, zeros(HEADS, HEAD, HEAD), zeros(S5_GROUPS, S5_STATE),
                  zeros(S5_GROUPS, S5_STATE), zeros(D_C), zeros(CONV_W - 1, D_C))
    res_s = trunk(x_sample, state_hgrn, state_s5_re, state_s5_im, state_rglru, cache_conv)
    return (res_p[0], res_s[0]) + res_p[1:] + res_s[1:]
```

```python
import functools

import numpy as np
import jax
import jax.numpy as jnp
from jax import lax
from jax.experimental import pallas as pl
from jax.experimental.pallas import tpu as pltpu

F32 = jnp.float32
BF16 = jnp.bfloat16

D_MODEL = 1024
D_A = 384
D_B = 256
D_C = 384
HEAD = 64
HEADS = D_A // HEAD
PAIRS = HEADS // 2
PAIR_W = 2 * HEAD
S5_GROUPS = 16
S5_GROUP = 16
S5_STATE = 64
S5_W = S5_GROUPS * S5_STATE
LRU_HEADS = 6
LRU_BLOCK = 64
CONV_W = 4
LRU_C = 8.0
D_FF = 2816
D_IN = 4 * D_A + D_B + 2 * D_C
EPS = 1e-6

LANES = 128
UNIT = 64
SEQS = 4
ROWS = UNIT * SEQS
SUB = 16
CONV_PAD = 16
MM_CHUNK = 256
EARLY_CHUNKS = 2
S5_TILES = S5_W // LANES
LRU_TILES = D_C // LANES

COL_Q, COL_F, COL_V, COL_G = 0, D_A, 2 * D_A, 3 * D_A
COL_U = 4 * D_A
COL_XR = COL_U + D_B
COL_GR = COL_XR + D_C

VMEM_LIMIT = 60 * 1024 * 1024


def _dot(a, b):
    return jnp.dot(a, b, preferred_element_type=F32)


def _dot_nt(a, b):
    return lax.dot_general(a, b, (((1,), (1,)), ((), ())), preferred_element_type=F32)


def _dot_tn(a, b):
    return lax.dot_general(a, b, (((0,), (0,)), ((), ())), preferred_element_type=F32)


def _neg_expm1(x):
    u = jnp.exp(x)
    near = jnp.where(u == 1.0, -x, (1.0 - u) * x / jnp.log(u))
    return jnp.where(x < -0.5, 1.0 - u, near)


def _rms(x, w):
    return x * lax.rsqrt(jnp.mean(x * x, axis=-1, keepdims=True) + EPS) * w


def _layer_spec(arr, layer):
    nd = arr.ndim
    return pl.BlockSpec((None,) + arr.shape[1:], lambda *_: (layer,) + (0,) * (nd - 1),
                        pipeline_mode=pl.Buffered(1))


def _prep_kernel(lb_ref, lam_re_ref, lam_im_ref, log_dt_ref, bt_re_ref, bt_im_ref, lru_lam_ref,
                 lbs_ref, a_re_ref, a_im_ref, bbar_re_ref, bbar_im_ref, lru_c_ref):
    depth = lb_ref.shape[0]
    rows = [lb_ref[l:l + 1, :] for l in range(depth)]
    m = rows[0]
    for r in rows[1:]:
        m = jnp.maximum(m, r)
    es = [jnp.exp(r - m) for r in rows]
    tot = es[0]
    for e in es[1:]:
        tot = tot + e
    cs = None
    for l in range(depth):
        sm = es[l] / tot
        cs = sm if cs is None else cs + sm
        if l == 0:
            cs0 = cs
        lbs_ref[l:l + 1, :] = cs - cs0

    dt = jnp.exp(log_dt_ref[...])
    lr = jnp.minimum(lam_re_ref[...], -1e-4)
    li = lam_im_ref[...]
    mag = jnp.exp(lr * dt)
    a_re = mag * jnp.cos(li * dt)
    a_im = mag * jnp.sin(li * dt)
    den = lr * lr + li * li
    nr = a_re - 1.0
    gam_re = (nr * lr + a_im * li) / den
    gam_im = (a_im * lr - nr * li) / den
    a_re_ref[...] = a_re
    a_im_ref[...] = a_im
    b_re = bt_re_ref[...]
    b_im = bt_im_ref[...]
    bbar_re_ref[...] = gam_re * b_re - gam_im * b_im
    bbar_im_ref[...] = gam_re * b_im + gam_im * b_re

    z = -lru_lam_ref[...]
    softplus = jnp.maximum(z, 0.0) + jnp.log1p(jnp.exp(-jnp.abs(z)))
    lru_c_ref[...] = -LRU_C * softplus


def _prep(hgrn_lb, lam_re, lam_im, log_dt, b_re, b_im, lru_lam):
    depth = hgrn_lb.shape[0]
    g, p, n = S5_GROUPS, S5_STATE, S5_GROUP
    out_shape = (
        jax.ShapeDtypeStruct((depth, D_A), F32),
        jax.ShapeDtypeStruct((depth, g, 1, p), F32),
        jax.ShapeDtypeStruct((depth, g, 1, p), F32),
        jax.ShapeDtypeStruct((depth, g, n, p), F32),
        jax.ShapeDtypeStruct((depth, g, n, p), F32),
        jax.ShapeDtypeStruct((depth, D_C), F32),
    )
    return pl.pallas_call(_prep_kernel, out_shape=out_shape, name="prep")(
        hgrn_lb, lam_re.reshape(depth, g, 1, p), lam_im.reshape(depth, g, 1, p),
        log_dt.reshape(depth, g, 1, 1), jnp.swapaxes(b_re, -1, -2), jnp.swapaxes(b_im, -1, -2),
        lru_lam)


def _split3(x):
    hi = x.astype(BF16)
    r1 = x - hi.astype(F32)
    mid = r1.astype(BF16)
    lo = (r1 - mid.astype(F32)).astype(BF16)
    return hi, mid, lo


def _split2(x):
    hi = x.astype(BF16)
    return hi, (x - hi.astype(F32)).astype(BF16)


def _bcast_rows(x, idxs, reps):
    width = x.shape[1]
    return jnp.concatenate(
        [jnp.broadcast_to(x[i:i + 1, :], (reps, width)) for i in idxs], axis=0)


def _lane_tiles(x):
    return [x[:, j * LANES:(j + 1) * LANES] for j in range(x.shape[1] // LANES)]


N_STATES = 5


def _layer_kernel(steps_per_seq, final, zero_init, aliased, *refs):
    refs = iter(refs)
    take = lambda n: [next(refs) for _ in range(n)]
    x0_ref, xnext_ref, xprev_ref = take(3)
    if not zero_init:
        hg0_ref, s5r0_ref, s5i0_ref, lru0_ref, cv0_ref = take(N_STATES)
    take(aliased)
    (nmix_ref, win_ref,
     lb_ref, hnorm_ref, tri_ref, ones_ref, perm_ref, permt_ref,
     wb_ref, are_ref, aim_ref, wc_ref, d_ref, wglu_ref, bglu_ref, snorm_ref,
     cw_ref, cb_ref, wax_ref, bax_ref, lc_ref, lnorm_ref,
     wo_ref, nffn_ref, wg_ref, wu_ref, wd_ref, nfin_ref) = take(28)
    out_ref, hgo_ref, s5ro_ref, s5io_ref, lruo_ref, cvo_ref = take(1 + N_STATES)
    (st_hg, st_s5, st_lru, st_cv, proj_scr, pnext_scr, bu_scr, la_scr, lb_scr, xp_scr,
     o_scr, mix_scr, x1_scr, xn_scr, hid_scr) = take(15)
    step = pl.program_id(0) % steps_per_seq

    @pl.when(pl.program_id(0) == 0)
    def _():
        xn = _rms(x0_ref[...].reshape(ROWS, D_MODEL), nmix_ref[...])
        pnext_scr[...] = _dot(xn.astype(BF16), win_ref[...])
        mix_scr[...] = jnp.zeros((ROWS, D_MODEL), F32)

    zero_head = jnp.zeros((HEAD, HEAD), F32)

    @pl.when(step == 0)
    def _():
        if zero_init:
            st_hg[...] = jnp.zeros(st_hg.shape, F32)
            st_s5[...] = jnp.zeros(st_s5.shape, F32)
            st_lru[...] = jnp.zeros(st_lru.shape, F32)
            st_cv[...] = jnp.zeros(st_cv.shape, F32)
            return
        for u in range(SEQS):
            for p in range(PAIRS):
                st_hg[u, p] = jnp.concatenate([
                    jnp.concatenate([hg0_ref[u, 2 * p], zero_head], axis=1),
                    jnp.concatenate([zero_head, hg0_ref[u, 2 * p + 1]], axis=1)], axis=0)
            for j in range(CONV_W - 1):
                st_cv[j * SEQS + u:j * SEQS + u + 1, :] = cv0_ref[u, j:j + 1, :]
            for g in range(S5_GROUPS):
                lanes = slice(g * S5_STATE, (g + 1) * S5_STATE)
                st_s5[u:u + 1, lanes] = s5r0_ref[u, g:g + 1, :]
                st_s5[u:u + 1, S5_W + g * S5_STATE:S5_W + (g + 1) * S5_STATE] = s5i0_ref[u, g:g + 1, :]
        st_lru[...] = jnp.concatenate([lru0_ref[u] for u in range(SEQS)], axis=0)

    def neighbour_stream():
        chunk = lambda c: slice(c * MM_CHUNK, (c + 1) * MM_CHUNK)
        mix_prev = mix_scr[...].astype(BF16)

        def out_projection(c):
            x1_scr[:, chunk(c)] = (xprev_ref[:, :, chunk(c)].reshape(ROWS, MM_CHUNK)
                                   + _dot(mix_prev, wo_ref[:, chunk(c)]))

        for c in range(EARLY_CHUNKS):
            out_projection(c)
            yield
        x_next = _rms(xnext_ref[...].reshape(ROWS, D_MODEL), nmix_ref[...]).astype(BF16)
        for c in range(D_IN // MM_CHUNK):
            pnext_scr[:, chunk(c)] = _dot(x_next, win_ref[:, chunk(c)])
            yield
        for c in range(EARLY_CHUNKS, D_MODEL // MM_CHUNK):
            out_projection(c)
            yield
        xn_scr[...] = _rms(x1_scr[...], nffn_ref[...]).astype(BF16)
        yield
        for c in range(D_FF // MM_CHUNK):
            xn = xn_scr[...]
            gate = _dot(xn, wg_ref[:, chunk(c)])
            yield
            up = _dot(xn, wu_ref[:, chunk(c)])
            hid_scr[:, chunk(c)] = (jax.nn.silu(gate) * up).astype(BF16)
            yield
        for c in range(D_MODEL // MM_CHUNK):
            x2 = x1_scr[:, chunk(c)] + _dot(hid_scr[...], wd_ref[:, chunk(c)])
            if final:
                x1_scr[:, chunk(c)] = x2
            else:
                out_ref[:, :, chunk(c)] = x2.reshape(SEQS, UNIT, MM_CHUNK)
            yield
        if final:
            out_ref[...] = _rms(x1_scr[...], nfin_ref[...]).reshape(SEQS, UNIT, D_MODEL)

    stream = neighbour_stream()

    def neighbours(n):
        for _ in range(n):
            next(stream, None)

    def cols(lo, width):
        return proj_scr[:, lo:lo + width]

    neighbours(EARLY_CHUNKS)
    proj_scr[:, 0:COL_U] = pnext_scr[:, 0:COL_U]
    proj_scr[:, COL_GR:D_IN] = pnext_scr[:, COL_GR:D_IN]
    tm_parts = jnp.concatenate(_split2(pnext_scr[:, COL_U:COL_GR]), axis=0)
    neighbours(1)
    tm = _dot(perm_ref[...], tm_parts)
    neighbours(1)
    u_in = tm[:, 0:D_B]
    xr = tm[:, D_B:D_B + D_C]
    frame_rows = lambda t: slice(t * SEQS, (t + 1) * SEQS)

    def scan(step_fn, carry, gap):
        for t in range(UNIT):
            carry = step_fn(t, carry)
            if t % gap == gap - 1:
                neighbours(1)
        return carry

    bu = _dot(u_in.astype(BF16), wb_ref[...])
    neighbours(1)
    for j, tile in enumerate(_lane_tiles(bu)):
        bu_scr[j] = tile
    neighbours(2)
    a_re = [jnp.broadcast_to(a, (SEQS, LANES)) for a in _lane_tiles(are_ref[...])]
    a_im = [jnp.broadcast_to(a, (SEQS, LANES)) for a in _lane_tiles(aim_ref[...])]

    def s5_step(t, h):
        rows = frame_rows(t)
        new_re, new_im = [], []
        for j in range(S5_TILES):
            hr, hi = h[0][j], h[1][j]
            nr = a_re[j] * hr - a_im[j] * hi + bu_scr[j, rows, :]
            ni = a_re[j] * hi + a_im[j] * hr + bu_scr[S5_TILES + j, rows, :]
            bu_scr[j, rows, :] = nr
            bu_scr[S5_TILES + j, rows, :] = ni
            new_re.append(nr)
            new_im.append(ni)
        return tuple(new_re), tuple(new_im)

    h0 = _lane_tiles(st_s5[...])
    h_re, h_im = scan(s5_step, (tuple(h0[:S5_TILES]), tuple(h0[S5_TILES:])), UNIT // 4)
    st_s5[...] = jnp.concatenate(list(h_re) + list(h_im), axis=1)

    h_all = jnp.concatenate([bu_scr[j] for j in range(2 * S5_TILES)], axis=1)
    y = _dot(h_all.astype(BF16), wc_ref[...]) + d_ref[...] * u_in
    neighbours(2)
    z = jax.nn.gelu(y)
    z = z * jax.nn.sigmoid(_dot(z.astype(BF16), wglu_ref[...]) + bglu_ref[...])
    neighbours(2)
    out_b = _rms(z, snorm_ref[...])

    tail = (CONV_W - 1) * SEQS
    xp_scr[CONV_PAD - tail:CONV_PAD, :] = st_cv[...]
    xp_scr[CONV_PAD:CONV_PAD + ROWS, :] = xr
    xc = cb_ref[...] + xr * cw_ref[CONV_W - 1:CONV_W, :]
    for j in range(CONV_W - 1):
        lo = CONV_PAD - tail + j * SEQS
        xc = xc + xp_scr[lo:lo + ROWS, :] * cw_ref[j:j + 1, :]
    st_cv[...] = xr[ROWS - tail:ROWS, :]
    gates = _dot(xc.astype(BF16), wax_ref[...]) + bax_ref[...]
    neighbours(2)
    r_gate = jax.nn.sigmoid(gates[:, 0:D_C])
    i_gate = jax.nn.sigmoid(gates[:, D_C:2 * D_C])
    log_a = lc_ref[...] * r_gate
    for j, tile in enumerate(_lane_tiles(jnp.exp(log_a))):
        la_scr[j] = tile
    neighbours(2)
    for j, tile in enumerate(_lane_tiles(jnp.sqrt(_neg_expm1(2.0 * log_a)) * (i_gate * xc))):
        lb_scr[j] = tile
    neighbours(1)

    def lru_step(t, h):
        rows = frame_rows(t)
        new = []
        for j in range(LRU_TILES):
            nh = la_scr[j, rows, :] * h[j] + lb_scr[j, rows, :]
            lb_scr[j, rows, :] = nh
            new.append(nh)
        return tuple(new)

    h_lru = scan(lru_step, tuple(_lane_tiles(st_lru[...])), UNIT // 2)
    st_lru[...] = jnp.concatenate(list(h_lru), axis=1)
    scanned = jnp.concatenate([out_b] + [lb_scr[j] for j in range(LRU_TILES)], axis=1)
    back = _dot(permt_ref[...], jnp.concatenate(_split2(scanned), axis=0))
    mix_scr[:, D_A:D_A + D_B] = back[:, 0:D_B]
    yc = back[:, D_B:D_B + D_C] * jax.nn.gelu(cols(COL_GR, D_C))
    mix_scr[:, D_A + D_B:D_MODEL] = _rms(yc, lnorm_ref[...])

    units = [slice(u * UNIT, (u + 1) * UNIT) for u in range(SEQS)]
    pairs = [slice(p * PAIR_W, (p + 1) * PAIR_W) for p in range(PAIRS)]
    lbv = lb_ref[...]
    rowi = lax.broadcasted_iota(jnp.int32, (ROWS, D_A), 0) % UNIT
    right0 = (rowi % (2 * SUB)) >= SUB
    right1 = rowi >= 2 * SUB
    head0 = lax.broadcasted_iota(jnp.int32, (ROWS, D_A), 1) % PAIR_W < HEAD
    tq = lax.broadcasted_iota(jnp.int32, (UNIT, PAIR_W), 0)
    sk = lax.broadcasted_iota(jnp.int32, (UNIT, PAIR_W), 1) % HEAD
    lag = tq - sk
    mask_d = lag * (lag - tq % SUB) <= 0
    mask_0 = (tq // (2 * SUB)) == (sk // (2 * SUB))
    ki = lax.broadcasted_iota(jnp.int32, (PAIR_W, PAIR_W), 0) // HEAD
    vi = lax.broadcasted_iota(jnp.int32, (PAIR_W, PAIR_W), 1) // HEAD
    same_head = ki == vi

    def by_head(x):
        return jnp.where(head0, x, 0.0).astype(BF16), jnp.where(head0, 0.0, x).astype(BF16)

    def stacked(halves, u, p):
        return jnp.concatenate([halves[0][units[u], pairs[p]], halves[1][units[u], pairs[p]]],
                               axis=0)

    neighbours(3)
    q = cols(COL_Q, D_A)
    fr = cols(COL_F, D_A)
    e = jnp.exp(-jnp.abs(fr))
    rcp = 1.0 / (1.0 + e)
    pos = fr >= 0.0
    sig = jnp.where(pos, rcp, e * rcp)
    sgm = jnp.where(pos, e * rcp, rcp)
    lf3 = _split3(jnp.log(lbv + (1.0 - lbv) * sig))
    k = (1.0 - lbv) * sgm
    g_cum = jnp.concatenate(
        [_dot(tri_ref[...], jnp.concatenate([part[rows] for part in lf3], axis=0))
         for rows in units], axis=0)
    neighbours(2)
    mids = [u * UNIT + SUB * i + SUB // 2 - 1 for u in range(SEQS) for i in range(UNIT // SUB)]
    dd = g_cum - _bcast_rows(g_cum, mids, SUB)
    q_d = (q * jnp.exp(dd)).astype(BF16)
    k_d = by_head(k * jnp.exp(-dd))
    neighbours(2)
    bounds = [u * UNIT + 2 * SUB * i + SUB - 1 for u in range(SEQS)
              for i in range(UNIT // (2 * SUB))]
    x0 = g_cum - _bcast_rows(g_cum, bounds, 2 * SUB)
    e0 = jnp.exp(jnp.where(right0, x0, -x0))
    q_0 = jnp.where(right0, q * e0, 0.0).astype(BF16)
    k_0 = by_head(jnp.where(right0, 0.0, k * e0))
    neighbours(2)
    x1 = g_cum - _bcast_rows(g_cum, [u * UNIT + 2 * SUB - 1 for u in range(SEQS)], UNIT)
    e1 = jnp.exp(jnp.where(right1, x1, -x1))
    q_1 = jnp.where(right1, q * e1, 0.0).astype(BF16)
    k_1 = by_head(jnp.where(right1, 0.0, k * e1))
    neighbours(2)
    g_end = _bcast_rows(g_cum, [u * UNIT + UNIT - 1 for u in range(SEQS)], UNIT)
    q_h = (q * jnp.exp(g_cum)).astype(BF16)
    k_h = (k * jnp.exp(g_end - g_cum)).astype(BF16)
    v = cols(COL_V, D_A)
    vb = v.astype(BF16)
    v_heads = by_head(v)
    neighbours(2)
    up = [(u, p) for u in range(SEQS) for p in range(PAIRS)]
    scores_d = [_dot_nt(q_d[units[u], pairs[p]], stacked(k_d, u, p)) for u, p in up]
    neighbours(1)
    scores_0 = [_dot_nt(q_0[units[u], pairs[p]], stacked(k_0, u, p)) for u, p in up]
    neighbours(1)
    scores_1 = [_dot_nt(q_1[units[u], pairs[p]], stacked(k_1, u, p)) for u, p in up]
    att = [(jnp.where(mask_d, s_d, 0.0) + jnp.where(mask_0, s_0, 0.0) + s_1).astype(BF16)
           for s_d, s_0, s_1 in zip(scores_d, scores_0, scores_1)]
    neighbours(2)
    states = [st_hg[u, p] for u, p in up]
    outs = [_dot(a, stacked(v_heads, u, p)) + _dot(q_h[units[u], pairs[p]], st.astype(BF16))
            for (u, p), a, st in zip(up, att, states)]
    updates = [_dot_tn(k_h[units[u], pairs[p]], vb[units[u], pairs[p]]) for u, p in up]
    for (u, p), o_up, st, upd in zip(up, outs, states, updates):
        o_scr[units[u], pairs[p]] = o_up
        dec = jnp.exp(g_cum[u * UNIT + UNIT - 1:u * UNIT + UNIT, pairs[p]])
        decay = jnp.broadcast_to(dec, (PAIR_W, PAIR_W)).T
        st_hg[u, p] = st * decay + jnp.where(same_head, upd, 0.0)
    o = o_scr[...]
    o2 = o * o
    o2_hi = o2.astype(BF16)
    o2_lo = (o2 - o2_hi.astype(F32)).astype(BF16)
    msq = _dot(jnp.concatenate([o2_hi, o2_lo], axis=1), ones_ref[...]) * (1.0 / HEAD)
    on = o * lax.rsqrt(msq + EPS)
    mix_scr[:, 0:D_A] = on * hnorm_ref[...] * jax.nn.silu(cols(COL_G, D_A))
    for _ in stream:
        pass

    @pl.when(step == steps_per_seq - 1)
    def _():
        for u in range(SEQS):
            for p in range(PAIRS):
                st = st_hg[u, p]
                hgo_ref[u, 2 * p] = st[0:HEAD, 0:HEAD]
                hgo_ref[u, 2 * p + 1] = st[HEAD:PAIR_W, HEAD:PAIR_W]
            for g in range(S5_GROUPS):
                lanes = slice(g * S5_STATE, (g + 1) * S5_STATE)
                s5ro_ref[u, g:g + 1, :] = st_s5[u:u + 1, lanes]
                s5io_ref[u, g:g + 1, :] = st_s5[u:u + 1, S5_W + g * S5_STATE:S5_W + (g + 1) * S5_STATE]
            lruo_ref[u] = st_lru[u:u + 1, :]
            for j in range(CONV_W - 1):
                cvo_ref[u, j:j + 1, :] = st_cv[j * SEQS + u:j * SEQS + u + 1, :]


def _layer_constants():
    t = np.arange(UNIT)
    tri = (t[None, :] <= t[:, None]).astype(np.float32)
    tri3 = np.concatenate([tri, tri, tri], axis=1)
    h = np.arange(D_A) // HEAD
    ones = (h[:, None] == h[None, :]).astype(np.float32)
    ones2 = np.concatenate([ones, ones], axis=0)
    r = np.arange(ROWS)
    src = (r % SEQS) * UNIT + r // SEQS
    perm = (src[:, None] == r[None, :]).astype(np.float32)
    perm2 = np.concatenate([perm, perm], axis=1)
    permt2 = np.concatenate([perm.T, perm.T], axis=1)
    return [jnp.asarray(c, BF16) for c in (tri3, ones2, perm2, permt2)]


def _state_shapes(depth, nseq):
    return [(depth, nseq, HEADS, HEAD, HEAD), (depth, nseq, S5_GROUPS, S5_STATE),
            (depth, nseq, S5_GROUPS, S5_STATE), (depth, nseq, 1, D_C),
            (depth, nseq, CONV_W - 1, D_C)]


def _layer(x, init_states, state_outs, weights, params, norm_final, layer, final):
    nseq, length, _ = x.shape
    depth = weights["w_in"].shape[0]
    steps_per_seq = length // UNIT
    steps = (nseq // SEQS) * steps_per_seq
    constants = _layer_constants()
    states = [] if init_states is None else list(init_states)
    carried = [] if state_outs is None else list(state_outs)

    def block_map(i):
        return (i // steps_per_seq, i % steps_per_seq, 0)

    def group(i):
        return jnp.minimum(i, steps - 1) // steps_per_seq

    def state_spec(shape):
        nd = len(shape) - 2
        return pl.BlockSpec((None, SEQS) + tuple(shape[2:]),
                            lambda i: (layer, group(i)) + (0,) * nd)

    def const_spec(arr):
        nd = arr.ndim
        return pl.BlockSpec(arr.shape, lambda i: (0,) * nd, pipeline_mode=pl.Buffered(1))

    block = (SEQS, UNIT, D_MODEL)
    head = ["lb", "hnorm"]
    rest = ["wb", "a_re", "a_im", "wc", "d", "wglu", "bglu", "snorm",
            "cw", "cb", "wax", "bax", "lc", "lnorm"]
    ffn = ["w_out", "norm_ffn", "w_gate", "w_up", "w_down"]
    args = ([x, x, x] + states + carried + [weights["norm_mix"], weights["w_in"]]
            + [params[n] for n in head] + constants + [params[n] for n in rest]
            + [weights[n] for n in ffn] + [norm_final])
    in_specs = ([pl.BlockSpec(block, lambda i: (0, 0, 0), pipeline_mode=pl.Buffered(1)),
                 pl.BlockSpec(block, lambda i: block_map(jnp.minimum(i + 1, steps - 1))),
                 pl.BlockSpec(block, lambda i: block_map(jnp.maximum(i - 1, 0)))]
                + [state_spec(s.shape) for s in states]
                + [pl.BlockSpec(memory_space=pl.ANY) for _ in carried]
                + [_layer_spec(weights["norm_mix"], layer), _layer_spec(weights["w_in"], layer)]
                + [_layer_spec(params[n], layer) for n in head]
                + [const_spec(c) for c in constants]
                + [_layer_spec(params[n], layer) for n in rest]
                + [_layer_spec(weights[n], layer) for n in ffn]
                + [const_spec(norm_final)])
    state_shapes = _state_shapes(depth, nseq)
    out_shape = ((jax.ShapeDtypeStruct((nseq, length, D_MODEL), F32),)
                 + tuple(jax.ShapeDtypeStruct(s, F32) for s in state_shapes))
    first_carried = 3 + len(states)
    aliases = {first_carried + k: 1 + k for k in range(len(carried))}
    scratch = [
        pltpu.VMEM((SEQS, PAIRS, PAIR_W, PAIR_W), F32),
        pltpu.VMEM((SEQS, 2 * S5_W), F32),
        pltpu.VMEM((SEQS, D_C), F32),
        pltpu.VMEM(((CONV_W - 1) * SEQS, D_C), F32),
        pltpu.VMEM((ROWS, D_IN), F32),
        pltpu.VMEM((ROWS, D_IN), F32),
        pltpu.VMEM((2 * S5_TILES, ROWS, LANES), F32),
        pltpu.VMEM((LRU_TILES, ROWS, LANES), F32),
        pltpu.VMEM((LRU_TILES, ROWS, LANES), F32),
        pltpu.VMEM((CONV_PAD + ROWS, D_C), F32),
        pltpu.VMEM((ROWS, D_A), F32),
        pltpu.VMEM((ROWS, D_MODEL), F32),
        pltpu.VMEM((ROWS, D_MODEL), F32),
        pltpu.VMEM((ROWS, D_MODEL), BF16),
        pltpu.VMEM((ROWS, D_FF), BF16),
    ]
    return pl.pallas_call(
        functools.partial(_layer_kernel, steps_per_seq, final, init_states is None, len(carried)),
        grid=(steps + 1,),
        in_specs=in_specs,
        out_specs=([pl.BlockSpec(block, lambda i: block_map(jnp.maximum(i - 1, 0)))]
                   + [state_spec(s) for s in state_shapes]),
        out_shape=out_shape,
        input_output_aliases=aliases,
        scratch_shapes=scratch,
        compiler_params=pltpu.CompilerParams(
            dimension_semantics=("arbitrary",), vmem_limit_bytes=VMEM_LIMIT),
        name="layer",
    )(*args)


def _block_diag(w):
    l, h, i, j = w.shape
    return jnp.einsum("lhij,hk->lhikj", w, jnp.eye(h, dtype=w.dtype)).reshape(l, h * i, h * j)


def _mixer_params(consts, s5_c_re, s5_c_im, s5_d, s5_w_glu, s5_b_glu, s5_norm, hgrn_norm,
                  lru_conv_w, lru_conv_b, lru_wa, lru_ba, lru_wx, lru_bx, lru_norm):
    lbs, a_re, a_im, bbar_re, bbar_im, lru_c = consts
    depth = lbs.shape[0]
    eye = jnp.eye(S5_GROUPS, dtype=F32)
    row = lambda x, w: x.reshape(depth, 1, w)
    embed_b = lambda b: jnp.einsum("lgnp,gh->lgnhp", b, eye).reshape(depth, D_B, S5_W)
    embed_c = lambda c: jnp.einsum("lgnp,gh->lgphn", c, eye).reshape(depth, S5_W, D_B)
    return {
        "lb": row(lbs, D_A), "hnorm": row(hgrn_norm, D_A),
        "wb": jnp.concatenate([embed_b(bbar_re), embed_b(bbar_im)], axis=2).astype(BF16),
        "a_re": row(a_re, S5_W), "a_im": row(a_im, S5_W),
        "wc": jnp.concatenate([embed_c(s5_c_re), embed_c(-s5_c_im)], axis=1).astype(BF16),
        "d": row(s5_d, D_B), "wglu": s5_w_glu.astype(BF16), "bglu": row(s5_b_glu, D_B),
        "snorm": row(s5_norm, D_B), "cw": lru_conv_w, "cb": row(lru_conv_b, D_C),
        "wax": jnp.concatenate([_block_diag(lru_wa), _block_diag(lru_wx)], axis=2).astype(BF16),
        "bax": jnp.concatenate([lru_ba, lru_bx], axis=1).reshape(depth, 1, 2 * D_C),
        "lc": row(lru_c, D_C), "lnorm": row(lru_norm, D_C),
    }


def kernel(x_prompt, x_sample, state_hgrn, state_s5_re, state_s5_im, state_rglru, cache_conv, norm_mix, w_in, hgrn_lb, hgrn_norm, s5_lam_re, s5_lam_im, s5_log_dt, s5_b_re, s5_b_im, s5_c_re, s5_c_im, s5_d, s5_w_glu, s5_b_glu, s5_norm, lru_conv_w, lru_conv_b, lru_wa, lru_ba, lru_wx, lru_bx, lru_lam, lru_norm, w_out, norm_ffn, w_ffn_gate, w_ffn_up, w_ffn_down, norm_final):
    depth = w_in.shape[0]
    consts = _prep(hgrn_lb, s5_lam_re, s5_lam_im, s5_log_dt, s5_b_re, s5_b_im, lru_lam)
    params = _mixer_params(consts, s5_c_re, s5_c_im, s5_d, s5_w_glu, s5_b_glu, s5_norm, hgrn_norm,
                           lru_conv_w, lru_conv_b, lru_wa, lru_ba, lru_wx, lru_bx, lru_norm)
    weights = {
        "norm_mix": norm_mix.reshape(depth, 1, D_MODEL), "w_in": w_in.astype(BF16),
        "w_out": w_out.astype(BF16), "norm_ffn": norm_ffn.reshape(depth, 1, D_MODEL),
        "w_gate": w_ffn_gate.astype(BF16), "w_up": w_ffn_up.astype(BF16),
        "w_down": w_ffn_down.astype(BF16),
    }
    norm_final2 = norm_final.reshape(1, D_MODEL)

    def trunk(x, init_states):
        nseq = x.shape[0]
        state_outs = None
        for l in range(depth):
            x, *state_outs = _layer(x, init_states, state_outs, weights, params, norm_final2, l,
                                    l == depth - 1)
        hg, s5_re, s5_im, lru, cv = state_outs
        return (x, hg, s5_re, s5_im, lru.reshape(depth, nseq, D_C), cv)

    res_p = trunk(x_prompt, None)
    nseq_s = x_sample.shape[0]
    res_s = trunk(x_sample, (state_hgrn, state_s5_re, state_s5_im,
                             state_rglru.reshape(depth, nseq_s, 1, D_C), cache_conv))
    return (res_p[0], res_s[0]) + res_p[1:] + res_s[1:]
```
